```python
import math
import jax
import jax.numpy as jnp
from jax import lax
import numpy as np

D_MODEL = 1024
BATCH = 2
SEQ = 16384
DEPTH = 1
DEC_BATCH = 128
DEC_SEQ = 4
PAST_LEN = 8192
PAGE_SIZE = 128

MIX_WIDTH = D_MODEL
M_WIDTH = MIX_WIDTH // 2
M_HEADS = 4
M_HEAD_DIM = M_WIDTH // M_HEADS
M_CHUNK = 64
N_WIDTH = MIX_WIDTH - M_WIDTH
N_HEADS = 8
N_HEAD_DIM = N_WIDTH // N_HEADS
N_KV = 2
N_REP = N_HEADS // N_KV
KV_WIDTH = 2 * N_KV * N_HEAD_DIM
BLOCK = 64
N_SELECT = 16
WINDOW = 512
Q_BLOCK = 128
RMS_EPS = 1e-6
PROJ_WIDTHS = (M_WIDTH, M_WIDTH, M_WIDTH, M_WIDTH, M_WIDTH, M_HEADS, M_HEADS,
               N_WIDTH, KV_WIDTH, KV_WIDTH, KV_WIDTH, 3 * N_HEADS, N_WIDTH)

kernel_name = 'hymba_mlstm_nsa_step'


def rmsnorm(x, g):
    xf = x.astype(jnp.float32)
    y = xf * lax.rsqrt(jnp.mean(xf * xf, axis=-1, keepdims=True) + RMS_EPS)
    return (y * g.astype(jnp.float32)).astype(x.dtype)


def masked_softmax(s, mask):
    s = jnp.where(mask, s, -jnp.inf)
    mx = jnp.max(s, axis=-1, keepdims=True)
    mx = jnp.where(jnp.isfinite(mx), mx, 0.0)
    e = jnp.where(mask, jnp.exp(s - mx), 0.0)
    tot = jnp.sum(e, axis=-1, keepdims=True)
    return e / jnp.where(tot > 0, tot, 1.0)


def split_proj(u):
    bounds = np.cumsum(np.array(PROJ_WIDTHS))[:-1].tolist()
    return jnp.split(u, bounds, axis=-1)


def mixer_inputs(x, g_pre, w_in, b_i, b_f):
    bsz, t = x.shape[:2]
    u = rmsnorm(x, g_pre) @ w_in
    (q_m, k_m, v_m, o_m, z_m, i_m, f_m, q_n, kv_c, kv_s, kv_w, g_n, z_n) = split_proj(u)
    heads = lambda a: a.reshape(bsz, t, M_HEADS, M_HEAD_DIM).astype(jnp.float32)
    kv = lambda a: a.reshape(bsz, t, 2, N_KV, N_HEAD_DIM)
    ig = i_m.astype(jnp.float32) + b_i.astype(jnp.float32)
    fg = f_m.astype(jnp.float32) + b_f.astype(jnp.float32)
    qg = q_n.reshape(bsz, t, N_KV, N_REP, N_HEAD_DIM)
    return (heads(q_m), heads(k_m), heads(v_m), o_m, z_m, ig, fg, qg,
            kv(kv_c), kv(kv_s), kv(kv_w), g_n, z_n)


def mlstm_chunk(carry, xs):
    c, n, m = carry
    q, k, v, ig, lf = xs
    length = q.shape[2]
    b = jnp.cumsum(lf, axis=-1)
    causal = jnp.tril(jnp.ones((length, length), bool))
    dmat = jnp.where(causal, b[..., :, None] - b[..., None, :] + ig[..., None, :], -jnp.inf)
    inter = b + m[..., None]
    mt = jnp.maximum(inter, jnp.max(dmat, axis=-1))
    w_d = jnp.exp(dmat - mt[..., None])
    w_i = jnp.exp(inter - mt)
    qk = jnp.einsum('bhtd,bhsd->bhts', q, k) * w_d
    num = w_i[..., None] * jnp.einsum('bhvd,bhtd->bhtv', c, q) + jnp.einsum('bhts,bhsv->bhtv', qk, v)
    den = w_i * jnp.einsum('bhd,bhtd->bht', n, q) + jnp.sum(qk, axis=-1)
    h = num / jnp.maximum(jnp.abs(den), jnp.exp(-mt))[..., None]
    w_last = w_d[..., -1, :]
    c_new = w_i[..., -1, None, None] * c + jnp.einsum('bhs,bhsv,bhsd->bhvd', w_last, v, k)
    n_new = w_i[..., -1, None] * n + jnp.einsum('bhs,bhsd->bhd', w_last, k)
    return (c_new, n_new, mt[..., -1]), h


def mlstm(q, k, v, ig, fg, c0, n0, m0):
    bsz, t = q.shape[:2]
    length = math.gcd(t, M_CHUNK)
    nc = t // length

    def to_chunks(a):
        a = a.reshape((bsz, nc, length) + a.shape[2:])
        return jnp.moveaxis(jnp.moveaxis(a, 3, 2), 1, 0)

    xs = (to_chunks(q), to_chunks(k * (M_HEAD_DIM ** -0.5)), to_chunks(v),
          to_chunks(ig), to_chunks(jax.nn.log_sigmoid(fg)))
    (c, n, m), h = lax.scan(mlstm_chunk, (c0, n0, m0), xs)
    h = jnp.swapaxes(jnp.moveaxis(h, 0, 1), 2, 3).reshape(bsz, t, M_HEADS, M_HEAD_DIM)
    return h, c, n, m


def pad_to_block(kv):
    pad = (-kv.shape[1]) % BLOCK
    return jnp.pad(kv, ((0, 0), (0, pad)) + ((0, 0),) * (kv.ndim - 2))


def compress(kv, w_ck, w_cv):
    bsz, length = kv.shape[:2]
    nb = length // BLOCK
    blk = kv.reshape(bsz, nb, BLOCK, 2, N_KV, N_HEAD_DIM)
    blk = blk.transpose(0, 1, 3, 4, 2, 5).reshape(bsz, nb, 2, N_KV, BLOCK * N_HEAD_DIM)
    return blk[:, :, 0] @ w_ck, blk[:, :, 1] @ w_cv


def sel_blocks(kv):
    bsz, length = kv.shape[:2]
    blk = kv.reshape(bsz, length // BLOCK, BLOCK, 2, N_KV, N_HEAD_DIM).transpose(0, 3, 4, 1, 2, 5)
    return blk[:, 0], blk[:, 1]


def cmp_sel_attend(qg, q_pos, ck, cv, sk, sv):
    scale = N_HEAD_DIM ** -0.5
    nb = ck.shape[1]
    blk_id = jnp.arange(nb, dtype=jnp.int32)
    complete = (blk_id[None, :] + 1) * BLOCK - 1 <= q_pos[:, None]
    s = jnp.einsum('bqgrd,bngd->bgrqn', qg, ck).astype(jnp.float32) * scale
    p = masked_softmax(s, complete)
    o_c = jnp.einsum('bgrqn,bngd->bqgrd', p.astype(cv.dtype), cv)
    imp = jnp.sum(p, axis=2)
    cur = (q_pos // BLOCK)[:, None]
    forced = ((blk_id == 0) | (blk_id == cur) | (blk_id == cur - 1)) & (blk_id <= cur)
    score = jnp.where(forced, jnp.inf, jnp.where(complete, imp, -jnp.inf))
    top, idx = lax.top_k(score, min(N_SELECT, nb))
    blk_ok = top > -jnp.inf
    gather = jax.vmap(jax.vmap(lambda a, i: a[i]))
    kb = gather(sk, idx)
    vb = gather(sv, idx)
    tok = idx[..., None] * BLOCK + jnp.arange(BLOCK, dtype=jnp.int32)
    tok_ok = blk_ok[..., None] & (tok <= q_pos[:, None, None])
    s2 = jnp.einsum('bqgrd,bgqnld->bgrqnl', qg, kb).astype(jnp.float32) * scale
    shp = s2.shape
    mask2 = tok_ok.reshape(tok_ok.shape[:3] + (-1,))[:, :, None]
    p2 = masked_softmax(s2.reshape(shp[:4] + (-1,)), mask2).reshape(shp)
    o_s = jnp.einsum('bgrqnl,bgqnld->bqgrd', p2.astype(vb.dtype), vb)
    return o_c, o_s


def window_attend(qg, q_pos, kv, k_pos):
    scale = N_HEAD_DIM ** -0.5
    dpos = q_pos[:, None] - k_pos[None, :]
    mask = (dpos >= 0) & (dpos < WINDOW) & (k_pos[None, :] >= 0)
    s = jnp.einsum('bqgrd,bkgd->bgrqk', qg, kv[:, :, 0]).astype(jnp.float32) * scale
    p = masked_softmax(s, mask).astype(kv.dtype)
    return jnp.einsum('bgrqk,bkgd->bqgrd', p, kv[:, :, 1])


def nsa_prompt(qg, kv_c, kv_s, kv_w, w_ck, w_cv):
    bsz, t = qg.shape[:2]
    ck, cv = compress(kv_c, w_ck, w_cv)
    sk, sv = sel_blocks(kv_s)
    kw_pad = jnp.pad(kv_w, ((0, 0), (WINDOW, 0), (0, 0), (0, 0), (0, 0)))

    def one_block(c):
        start = c * Q_BLOCK
        q_blk = lax.dynamic_slice_in_dim(qg, start, Q_BLOCK, axis=1)
        q_pos = start + jnp.arange(Q_BLOCK, dtype=jnp.int32)
        o_c, o_s = cmp_sel_attend(q_blk, q_pos, ck, cv, sk, sv)
        kw_blk = lax.dynamic_slice_in_dim(kw_pad, start, WINDOW + Q_BLOCK, axis=1)
        k_pos = start - WINDOW + jnp.arange(WINDOW + Q_BLOCK, dtype=jnp.int32)
        o_w = window_attend(q_blk, q_pos, kw_blk, k_pos)
        return jnp.stack([o_c, o_s, o_w])

    outs = lax.map(one_block, jnp.arange(t // Q_BLOCK, dtype=jnp.int32))
    return jnp.moveaxis(outs, 0, 2).reshape((3, bsz, t) + qg.shape[2:])


def mixer_output(x, h_m, o_m, z_m, branches, g_n, z_n, w_out, g_post):
    bsz, t = x.shape[:2]
    y_m = (jax.nn.sigmoid(o_m.astype(jnp.float32)) * h_m.reshape(bsz, t, M_WIDTH)).astype(x.dtype)
    y_m = y_m * jax.nn.silu(z_m)
    gates = jax.nn.sigmoid(g_n.astype(jnp.float32)).reshape(bsz, t, 3, N_KV, N_REP)
    gates = jnp.moveaxis(gates, 2, 0)[..., None]
    y_n = jnp.sum(gates * branches.astype(jnp.float32), axis=0).reshape(bsz, t, N_WIDTH).astype(x.dtype)
    y_n = y_n * jax.nn.silu(z_n)
    y = jnp.concatenate([y_m, y_n], axis=-1) @ w_out
    return x + rmsnorm(y, g_post)


def prompt_layer(x, g_pre, w_in, b_i, b_f, w_cmp_k, w_cmp_v, w_out, g_post):
    bsz, t = x.shape[:2]
    (q_m, k_m, v_m, o_m, z_m, ig, fg, qg, kv_c, kv_s, kv_w, g_n, z_n) = mixer_inputs(x, g_pre, w_in, b_i, b_f)
    c0 = jnp.zeros((bsz, M_HEADS, M_HEAD_DIM, M_HEAD_DIM), jnp.float32)
    n0 = jnp.zeros((bsz, M_HEADS, M_HEAD_DIM), jnp.float32)
    m0 = jnp.full((bsz, M_HEADS), -jnp.inf, jnp.float32)
    h_m, c, n, m = mlstm(q_m, k_m, v_m, ig, fg, c0, n0, m0)
    branches = nsa_prompt(qg, kv_c, kv_s, kv_w, w_cmp_k, w_cmp_v)
    y = mixer_output(x, h_m, o_m, z_m, branches, g_n, z_n, w_out, g_post)
    return y, (kv_c, kv_s, kv_w[:, -min(WINDOW, t):], c, n, m)


def sample_layer(x, cache_cmp, cache_sel, win_kv, c0, n0, m0, page_table,
                 g_pre, w_in, b_i, b_f, w_cmp_k, w_cmp_v, w_out, g_post):
    bsz, t = x.shape[:2]
    (q_m, k_m, v_m, o_m, z_m, ig, fg, qg, kv_c, kv_s, kv_w, g_n, z_n) = mixer_inputs(x, g_pre, w_in, b_i, b_f)
    h_m, c, n, m = mlstm(q_m, k_m, v_m, ig, fg, c0.astype(jnp.float32),
                         n0.astype(jnp.float32), m0.astype(jnp.float32))
    q_pos = PAST_LEN + jnp.arange(t, dtype=jnp.int32)

    def full_rows(cache, new):
        rows = cache[page_table].reshape((bsz, -1) + cache.shape[2:])
        return pad_to_block(jnp.concatenate([rows, new.astype(rows.dtype)], axis=1))

    ck, cv = compress(full_rows(cache_cmp, kv_c), w_cmp_k, w_cmp_v)
    sk, sv = sel_blocks(full_rows(cache_sel, kv_s))
    o_c, o_s = cmp_sel_attend(qg, q_pos, ck, cv, sk, sv)
    wb = win_kv.shape[1]
    win_all = jnp.concatenate([win_kv, kv_w.astype(win_kv.dtype)], axis=1)
    k_pos = PAST_LEN - wb + jnp.arange(wb + t, dtype=jnp.int32)
    o_w = window_attend(qg, q_pos, win_all, k_pos)
    branches = jnp.stack([o_c, o_s, o_w.astype(o_c.dtype)])
    y = mixer_output(x, h_m, o_m, z_m, branches, g_n, z_n, w_out, g_post)
    return y, (kv_c, kv_s, win_all[:, -wb:], c, n, m)


def setup_inputs(seed: int = 0) -> dict:
    key = jax.random.key(seed)
    ks = jax.random.split(key, 20)
    n_pages = PAST_LEN // PAGE_SIZE
    used = DEC_BATCH * n_pages
    n_phys = used + max(1, used // 4)
    wb = min(WINDOW, PAST_LEN)
    proj_out = sum(PROJ_WIDTHS)
    nrm = lambda k, shape, s: s * jax.random.normal(k, shape, jnp.float32)
    page_table = jax.random.permutation(ks[4], n_phys)[:used].reshape(DEC_BATCH, n_pages).astype(jnp.int32)
    return {
        'x_prompt': nrm(ks[0], (BATCH, SEQ, D_MODEL), 1.0),
        'x_sample': nrm(ks[1], (DEC_BATCH, DEC_SEQ, D_MODEL), 1.0),
        'cache_cmp_kv': nrm(ks[2], (DEPTH, n_phys, PAGE_SIZE, 2, N_KV, N_HEAD_DIM), 1.0),
        'cache_sel_kv': nrm(ks[3], (DEPTH, n_phys, PAGE_SIZE, 2, N_KV, N_HEAD_DIM), 1.0),
        'state_win_kv': nrm(ks[5], (DEPTH, DEC_BATCH, wb, 2, N_KV, N_HEAD_DIM), 1.0),
        'state_mlstm_c': nrm(ks[6], (DEPTH, DEC_BATCH, M_HEADS, M_HEAD_DIM, M_HEAD_DIM), 0.1),
        'state_mlstm_n': nrm(ks[7], (DEPTH, DEC_BATCH, M_HEADS, M_HEAD_DIM), 0.3),
        'state_mlstm_m': nrm(ks[8], (DEPTH, DEC_BATCH, M_HEADS), 0.5),
        'page_table': page_table,
        'g_pre': 1.0 + nrm(ks[9], (DEPTH, D_MODEL), 0.02),
        'w_in': nrm(ks[10], (DEPTH, D_MODEL, proj_out), D_MODEL ** -0.5),
        'b_i': nrm(ks[11], (DEPTH, M_HEADS), 0.1),
        'b_f': jnp.linspace(3.0, 6.0, M_HEADS, dtype=jnp.float32)[None] + nrm(ks[12], (DEPTH, M_HEADS), 0.1),
        'w_cmp_k': nrm(ks[13], (DEPTH, BLOCK * N_HEAD_DIM, N_HEAD_DIM), (BLOCK * N_HEAD_DIM) ** -0.5),
        'w_cmp_v': nrm(ks[14], (DEPTH, BLOCK * N_HEAD_DIM, N_HEAD_DIM), (BLOCK * N_HEAD_DIM) ** -0.5),
        'w_out': nrm(ks[15], (DEPTH, MIX_WIDTH, D_MODEL), MIX_WIDTH ** -0.5),
        'g_post': 1.0 + nrm(ks[16], (DEPTH, D_MODEL), 0.02),
    }


def reference(x_prompt, x_sample, cache_cmp_kv, cache_sel_kv, state_win_kv, state_mlstm_c,
              state_mlstm_n, state_mlstm_m, page_table, g_pre, w_in, b_i, b_f, w_cmp_k,
              w_cmp_v, w_out, g_post):
    xp, xs = x_prompt, x_sample
    new_p = [[] for _ in range(6)]
    new_s = [[] for _ in range(6)]
    for layer in range(DEPTH):
        prm = (g_pre[layer], w_in[layer], b_i[layer], b_f[layer], w_cmp_k[layer],
               w_cmp_v[layer], w_out[layer], g_post[layer])
        xp, st_p = prompt_layer(xp, *prm)
        xs, st_s = sample_layer(xs, cache_cmp_kv[layer], cache_sel_kv[layer], state_win_kv[layer],
                                state_mlstm_c[layer], state_mlstm_n[layer], state_mlstm_m[layer],
                                page_table, *prm)
        for lst, a in zip(new_p, st_p):
            lst.append(a)
        for lst, a in zip(new_s, st_s):
            lst.append(a)
    return (xp, xs,
            jnp.stack(new_p[0]), jnp.stack(new_p[1]), jnp.stack(new_p[2]),
            jnp.stack(new_p[3]), jnp.stack(new_p[4]), jnp.stack(new_p[5]),
            jnp.stack(new_s[0]), jnp.stack(new_s[1]), jnp.stack(new_s[2]),
            jnp.stack(new_s[3]), jnp.stack(new_s[4]), jnp.stack(new_s[5]))
```

```python
import functools
import math

import jax
import jax.numpy as jnp
from jax import lax
from jax.experimental import pallas as pl
from jax.experimental.pallas import tpu as pltpu

M_HEADS = 4
N_HEADS = 8
N_KV = 2
N_REP = N_HEADS // N_KV
BLOCK = 64
N_SELECT = 16
WINDOW = 512
Q_BLOCK = 128
RMS_EPS = 1e-6

LANES = 128
F32 = jnp.float32
BF16 = jnp.bfloat16
NEG_INF = float("-inf")

VMEM_LIMIT_BYTES = 56 * 1024 * 1024


def _cparams(*sem):
    return pltpu.CompilerParams(dimension_semantics=sem, vmem_limit_bytes=VMEM_LIMIT_BYTES)


def _dot(a, b):
    return jnp.dot(a, b, preferred_element_type=F32)


def _dot_nt(a, b):
    return lax.dot_general(a, b, (((1,), (1,)), ((), ())), preferred_element_type=F32)


def _dot_tn(a, b):
    return lax.dot_general(a, b, (((0,), (0,)), ((), ())), preferred_element_type=F32)


def _sigmoid(x):
    return 1.0 / (1.0 + jnp.exp(-x))


def _log_sigmoid(x):
    return jnp.minimum(x, 0.0) - jnp.log1p(jnp.exp(-jnp.abs(x)))


def _inproj_kernel(x_ref, g_ref, wn_ref, wt_ref, bn_ref, bt_ref,
                   qkv_ref, oz_ref, zn_ref, kvc_ref, kvs_ref, kvw_ref, small_ref,
                   ks_ref, kw_ref, qt_ref, vst_ref, vwt_ref, smallt_ref, *, mw, nw, kvw, scale):
    x = x_ref[...]
    ms = jnp.mean(x * x, axis=-1, keepdims=True)
    xn = ((x * lax.rsqrt(ms + RMS_EPS)) * g_ref[...]).astype(BF16)
    c = 0
    qkv_ref[...] = _dot(xn, wn_ref[:, c:c + 3 * mw])
    c += 3 * mw
    oz_ref[...] = _dot(xn, wn_ref[:, c:c + 2 * mw])
    c += 2 * mw
    zn_ref[...] = _dot(xn, wn_ref[:, c:c + nw])
    c += nw
    kvc_ref[...] = _dot(xn, wn_ref[:, c:c + kvw])
    c += kvw
    kvs = _dot(xn, wn_ref[:, c:c + kvw])
    kvs_ref[...] = kvs
    ks_ref[...] = kvs[:, :kvw // 2].astype(BF16)
    c += kvw
    kvwin = _dot(xn, wn_ref[:, c:c + kvw])
    kvw_ref[...] = kvwin
    kw_ref[...] = kvwin[:, :kvw // 2].astype(BF16)
    c += kvw
    small_ref[...] = _dot(xn, wn_ref[:, c:c + LANES]) + bn_ref[...]
    r = 0
    qt_ref[...] = (_dot_nt(wt_ref[r:r + nw, :], xn) * scale).astype(BF16)
    r += nw
    vst_ref[...] = _dot_nt(wt_ref[r:r + kvw // 2, :], xn).astype(BF16)
    r += kvw // 2
    vwt_ref[...] = _dot_nt(wt_ref[r:r + kvw // 2, :], xn).astype(BF16)
    r += kvw // 2
    smallt_ref[...] = _dot_nt(wt_ref[r:r + 32, :], xn) + bt_ref[...]


def _split_weights(w_in, b_i, b_f, mw, nw, kvw):
    d = w_in.shape[0]
    o = 0
    qkv_m = w_in[:, o:o + 3 * mw]; o += 3 * mw
    oz_m = w_in[:, o:o + 2 * mw]; o += 2 * mw
    w_i = w_in[:, o:o + M_HEADS]; o += M_HEADS
    w_f = w_in[:, o:o + M_HEADS]; o += M_HEADS
    q_n = w_in[:, o:o + nw]; o += nw
    kv_c = w_in[:, o:o + kvw]; o += kvw
    kv_s = w_in[:, o:o + kvw]; o += kvw
    kv_w = w_in[:, o:o + kvw]; o += kvw
    g_n = w_in[:, o:o + 3 * N_HEADS]; o += 3 * N_HEADS
    z_n = w_in[:, o:o + nw]; o += nw
    assert o == w_in.shape[1]
    n_small = 2 * M_HEADS + 3 * N_HEADS
    small = jnp.concatenate([w_i, w_f, g_n, jnp.zeros((d, LANES - n_small), w_in.dtype)], axis=1)
    wn = jnp.concatenate([qkv_m, oz_m, z_n, kv_c, kv_s, kv_w, small], axis=1).astype(BF16)
    wt = jnp.concatenate([q_n, kv_s[:, kvw // 2:], kv_w[:, kvw // 2:], small[:, :32]], axis=1).T.astype(BF16)
    bias = jnp.concatenate([b_i, b_f, jnp.zeros((LANES - 2 * M_HEADS,), F32)])
    return wn, wt, bias[None, :], bias[:32, None]


def _inproj(x2d, g_pre, wn, wt, bn, bt, *, mw, nw, kvw, tm):
    m, d = x2d.shape
    assert m % tm == 0 and tm % LANES == 0
    grid = (m // tm,)
    row = lambda width: pl.BlockSpec((tm, width), lambda i: (i, 0))
    col = lambda height: pl.BlockSpec((height, tm), lambda i: (0, i))
    full = lambda a: pl.BlockSpec(a.shape, lambda i: (0,) * a.ndim)
    g2 = g_pre[None, :]
    out_shape = (
        jax.ShapeDtypeStruct((m, 3 * mw), F32), jax.ShapeDtypeStruct((m, 2 * mw), F32),
        jax.ShapeDtypeStruct((m, nw), F32),
        jax.ShapeDtypeStruct((m, kvw), F32), jax.ShapeDtypeStruct((m, kvw), F32),
        jax.ShapeDtypeStruct((m, kvw), F32), jax.ShapeDtypeStruct((m, LANES), F32),
        jax.ShapeDtypeStruct((m, kvw // 2), BF16), jax.ShapeDtypeStruct((m, kvw // 2), BF16),
        jax.ShapeDtypeStruct((nw, m), BF16), jax.ShapeDtypeStruct((kvw // 2, m), BF16),
        jax.ShapeDtypeStruct((kvw // 2, m), BF16), jax.ShapeDtypeStruct((32, m), F32),
    )
    out_specs = (row(3 * mw), row(2 * mw), row(nw), row(kvw), row(kvw), row(kvw), row(LANES),
                 row(kvw // 2), row(kvw // 2), col(nw), col(kvw // 2), col(kvw // 2), col(32))
    hd = nw // N_HEADS
    return pl.pallas_call(
        functools.partial(_inproj_kernel, mw=mw, nw=nw, kvw=kvw, scale=hd ** -0.5),
        grid=grid,
        in_specs=[row(d), full(g2), full(wn), full(wt), full(bn), full(bt)],
        out_specs=out_specs, out_shape=out_shape,
        compiler_params=_cparams("arbitrary"), name="inproj",
    )(x2d, g2, wn, wt, bn, bt)


def _mlstm_kernel(qkv_ref, small_ref, smallt_ref, c0_ref, n0_ref, m0_ref,
                  h_ref, c_ref, n_ref, m_ref, c_s, n_s, m_s, *, chunk, dh):
    ci = pl.program_id(1)
    nh = M_HEADS
    mw = nh * dh

    @pl.when(ci == 0)
    def _():
        c_s[...] = c0_ref[0]
        n_s[...] = n0_ref[0]
        m_s[...] = m0_ref[0]

    t_idx = lax.broadcasted_iota(jnp.int32, (chunk, chunk), 0)
    s_idx = lax.broadcasted_iota(jnp.int32, (chunk, chunk), 1)
    causal = s_idx <= t_idx
    scale = dh ** -0.5
    for h in range(nh):
        q = qkv_ref[0, :, h * dh:(h + 1) * dh]
        k = qkv_ref[0, :, mw + h * dh:mw + (h + 1) * dh] * scale
        v = qkv_ref[0, :, 2 * mw + h * dh:2 * mw + (h + 1) * dh]
        ig_c = small_ref[0, :, h:h + 1]
        lf_c = _log_sigmoid(small_ref[0, :, nh + h:nh + h + 1])
        ig_r = smallt_ref[0, h:h + 1, :]
        lf_r = _log_sigmoid(smallt_ref[0, nh + h:nh + h + 1, :])
        b_c = jnp.sum(jnp.where(causal, lf_r, 0.0), axis=1, keepdims=True)
        b_r = jnp.sum(jnp.where(t_idx <= s_idx, lf_c, 0.0), axis=0, keepdims=True)
        m_prev = m_s[h:h + 1, 0:1]
        n_prev = n_s[h:h + 1, :]
        c_prev = c_s[h]
        inter = b_c + m_prev
        dmat = jnp.where(causal, (b_c - b_r) + ig_r, NEG_INF)
        mt = jnp.maximum(inter, jnp.max(dmat, axis=1, keepdims=True))
        w_d = jnp.exp(dmat - mt)
        w_i = jnp.exp(inter - mt)
        q_b = q.astype(BF16)
        k_b = k.astype(BF16)
        qk = _dot_nt(q_b, k_b) * w_d
        num = w_i * _dot_nt(q_b, c_prev.astype(BF16)) + _dot(qk.astype(BF16), v.astype(BF16))
        den = w_i * jnp.sum(q * n_prev, axis=1, keepdims=True) + jnp.sum(qk, axis=1, keepdims=True)
        h_ref[0, :, h * dh:(h + 1) * dh] = num / jnp.maximum(jnp.abs(den), jnp.exp(-mt))
        b_last = b_c[chunk - 1:chunk, :]
        m_last = mt[chunk - 1:chunk, :]
        wi_last = w_i[chunk - 1:chunk, :]
        w_last = jnp.exp(((b_last - b_c) + ig_c) - m_last)
        c_s[h] = wi_last * c_prev + _dot_tn((v * w_last).astype(BF16), k_b)
        n_s[h:h + 1, :] = wi_last * n_prev + jnp.sum(k * w_last, axis=0, keepdims=True)
        m_s[h:h + 1, :] = jnp.broadcast_to(m_last, (1, LANES))

    @pl.when(ci == pl.num_programs(1) - 1)
    def _():
        c_ref[0] = c_s[...]
        n_ref[0] = n_s[...]
        m_ref[0] = m_s[...]


def _mlstm(qkv, small, smallt, c0, n0, m0, *, chunk):
    s, t, mw3 = qkv.shape
    mw = mw3 // 3
    dh = mw // M_HEADS
    assert t % chunk == 0
    nc = t // chunk
    m0b = jnp.broadcast_to(m0[:, :, None], (s, M_HEADS, LANES))
    state = lambda *tail: pl.BlockSpec((1,) + tail, lambda b, c: (b,) + (0,) * len(tail))
    h, c, n, m = pl.pallas_call(
        functools.partial(_mlstm_kernel, chunk=chunk, dh=dh),
        grid=(s, nc),
        in_specs=[pl.BlockSpec((1, chunk, mw3), lambda b, c: (b, c, 0)),
                  pl.BlockSpec((1, chunk, LANES), lambda b, c: (b, c, 0)),
                  pl.BlockSpec((1, 8, chunk), lambda b, c: (b, 0, c)),
                  state(M_HEADS, dh, dh), state(M_HEADS, dh), state(M_HEADS, LANES)],
        out_specs=(pl.BlockSpec((1, chunk, mw), lambda b, c: (b, c, 0)),
                   state(M_HEADS, dh, dh), state(M_HEADS, dh), state(M_HEADS, LANES)),
        out_shape=(jax.ShapeDtypeStruct((s, t, mw), F32),
                   jax.ShapeDtypeStruct((s, M_HEADS, dh, dh), F32),
                   jax.ShapeDtypeStruct((s, M_HEADS, dh), F32),
                   jax.ShapeDtypeStruct((s, M_HEADS, LANES), F32)),
        scratch_shapes=[pltpu.VMEM((M_HEADS, dh, dh), F32), pltpu.VMEM((M_HEADS, dh), F32),
                        pltpu.VMEM((M_HEADS, LANES), F32)],
        compiler_params=_cparams("arbitrary", "arbitrary"), name="mlstm",
    )(qkv, small, smallt, c0, n0, m0b)
    return h, c, n, m[:, :, 0]


def _page_copies(table_ref, pages_ref, buf_ref, sem_ref, seq, slot, p, page):
    src = pages_ref.at[table_ref[seq, p]]
    return [pltpu.make_async_copy(src.at[:, pl.ds(half * LANES, LANES)],
                                  buf_ref.at[slot, half, pl.ds(p * page, page), :],
                                  sem_ref.at[slot])
            for half in range(2)]


def _start_pages(table_ref, pages_ref, buf_ref, sem_ref, seq, slot, n_pages, page):
    def body(p, carry):
        for cp in _page_copies(table_ref, pages_ref, buf_ref, sem_ref, seq, slot, p, page):
            cp.start()
        return carry
    lax.fori_loop(0, n_pages, body, 0)


def _wait_pages(table_ref, pages_ref, buf_ref, sem_ref, seq, slot, n_pages, page):
    def body(p, carry):
        for cp in _page_copies(table_ref, pages_ref, buf_ref, sem_ref, seq, slot, p, page):
            cp.wait()
        return carry
    lax.fori_loop(0, n_pages, body, 0)


def _compress_kernel(table_ref, pages_ref, bd_ref, out_ref, buf_ref, sem_ref, *, n_pages, page):
    b = pl.program_id(0)
    nseq = pl.num_programs(0)
    slot = lax.rem(b, 2)
    fetch = functools.partial(_start_pages, table_ref, pages_ref, buf_ref, sem_ref,
                              n_pages=n_pages, page=page)

    @pl.when(b == 0)
    def _():
        fetch(seq=b, slot=slot)

    @pl.when(b + 1 < nseq)
    def _():
        fetch(seq=b + 1, slot=1 - slot)

    _wait_pages(table_ref, pages_ref, buf_ref, sem_ref, b, slot, n_pages, page)
    nb = n_pages * page // BLOCK

    def body(l, acc):
        return tuple(
            acc[half] + _dot(buf_ref[slot, half, pl.ds(l, nb, stride=BLOCK), :].astype(BF16), bd_ref[half, l])
            for half in range(2))

    acc = lax.fori_loop(0, BLOCK, body, tuple(jnp.zeros((nb, LANES), F32) for _ in range(2)))
    for half in range(2):
        out_ref[0, :, half * LANES:(half + 1) * LANES] = acc[half]


def _compress_weights(w_ck, w_cv):
    hd = w_ck.shape[1]
    planes = []
    for w in (w_ck, w_cv):
        w3 = w.reshape(BLOCK, hd, hd)
        z = jnp.zeros_like(w3)
        rows = [jnp.concatenate([w3 if i == j else z for i in range(N_KV)], axis=2) for j in range(N_KV)]
        planes.append(jnp.concatenate(rows, axis=1))
    return jnp.stack(planes).astype(BF16)


def _compress(pages, table, bd):
    _, page, width = pages.shape
    assert width == 2 * LANES
    s, n_pages = table.shape
    nb = n_pages * page // BLOCK
    return pl.pallas_call(
        functools.partial(_compress_kernel, n_pages=n_pages, page=page),
        grid_spec=pltpu.PrefetchScalarGridSpec(
            num_scalar_prefetch=1, grid=(s,),
            in_specs=[pl.BlockSpec(memory_space=pl.ANY),
                      pl.BlockSpec(bd.shape, lambda b, t: (0, 0, 0, 0))],
            out_specs=pl.BlockSpec((1, nb, width), lambda b, t: (b, 0, 0)),
            scratch_shapes=[pltpu.VMEM((2, 2, n_pages * page, LANES), F32),
                            pltpu.SemaphoreType.DMA((2,))]),
        out_shape=jax.ShapeDtypeStruct((s, nb, width), F32),
        compiler_params=_cparams("arbitrary"), name="compress",
    )(table, pages, bd)


def _masked_softmax_cols(s, mask):
    s = jnp.where(mask, s, NEG_INF)
    mx = jnp.max(s, axis=0, keepdims=True)
    mx = jnp.where(mx == NEG_INF, 0.0, mx)
    e = jnp.where(mask, jnp.exp(s - mx), 0.0)
    tot = jnp.sum(e, axis=0, keepdims=True)
    return e / jnp.where(tot > 0, tot, 1.0)


def _select_blocks(imp, q_pos, n_valid):
    nb, nq = imp.shape
    nid = lax.broadcasted_iota(jnp.int32, (nb, nq), 0)
    cur = jnp.right_shift(q_pos, BLOCK.bit_length() - 1)
    complete = (nid + 1) * BLOCK - 1 <= q_pos
    forced = ((nid == 0) | (nid == cur) | (nid == cur - 1)) & (nid <= cur)
    score = jnp.where(forced, jnp.inf, jnp.where(complete, imp, NEG_INF))
    score = jnp.where(nid < n_valid, score, NEG_INF)

    def body(_, carry):
        score, sel = carry
        mx = jnp.max(score, axis=0, keepdims=True)
        first = jnp.min(jnp.where(score == mx, nid, nb), axis=0, keepdims=True)
        pick = nid == first
        sel = jnp.where(pick & (mx > NEG_INF), 1.0, sel)
        score = jnp.where(pick, NEG_INF, score)
        return score, sel

    _, sel = lax.fori_loop(0, min(N_SELECT, n_valid), body, (score, jnp.zeros((nb, nq), F32)))
    return sel


def _block_diag_queries(qt, hd):
    nq = qt.shape[1]
    z = jnp.zeros((hd, nq), qt.dtype)
    rows = []
    for g in range(N_KV):
        parts = []
        for gg in range(N_KV):
            for r in range(N_REP):
                parts.append(qt[(g * N_REP + r) * hd:(g * N_REP + r + 1) * hd, :] if gg == g else z)
        rows.append(jnp.concatenate(parts, axis=1))
    return jnp.concatenate(rows, axis=0)


KEY_TILE = 512


def _nsa_prompt_kernel(qt_ref, gt_ref, ck_ref, cvt_ref, ks_ref, vst_ref, kw_ref, vwt_ref,
                       out_ref, sel_ref, *, hd, nb):
    i = pl.program_id(1)
    nq = Q_BLOCK
    gl = N_REP * nq
    nl = N_KV * gl
    start = i * nq
    qbd = _block_diag_queries(qt_ref[...], hd)
    lane_q = lax.broadcasted_iota(jnp.int32, (1, nl), 1) & (nq - 1)
    q_pos = start + lane_q
    q_pos1 = start + lax.broadcasted_iota(jnp.int32, (1, nq), 1)

    s_c = _dot(ck_ref[0], qbd)
    nid = lax.broadcasted_iota(jnp.int32, (nb, nl), 0)
    p_c = _masked_softmax_cols(s_c, (nid + 1) * BLOCK - 1 <= q_pos)
    o_c, o_s, o_w = [], [], []
    for g in range(N_KV):
        pg = p_c[:, g * gl:(g + 1) * gl]
        o_c.append(_dot(cvt_ref[0, g * hd:(g + 1) * hd, :], pg.astype(BF16)))
        imp = pg[:, 0:nq]
        for r in range(1, N_REP):
            imp = imp + pg[:, r * nq:(r + 1) * nq]
        sel = _select_blocks(imp, q_pos1, nb)
        for r in range(N_REP):
            sel_ref[:, g * gl + r * nq:g * gl + (r + 1) * nq] = sel

    blocks_per_tile = KEY_TILE // BLOCK

    def tile(j, carry, diagonal):
        m_run, l_run, acc = carry
        off = pl.multiple_of(j * KEY_TILE, KEY_TILE)
        s = _dot(ks_ref[0, pl.ds(off, KEY_TILE), :], qbd)
        boff = pl.multiple_of(j * blocks_per_tile, blocks_per_tile)
        bias = jnp.where(sel_ref[pl.ds(boff, blocks_per_tile), :] > 0, 0.0, NEG_INF)
        parts = [s[n * BLOCK:(n + 1) * BLOCK, :] + bias[n:n + 1, :] for n in range(blocks_per_tile)]
        s = jnp.concatenate(parts, axis=0)
        if diagonal:
            k_pos = off + lax.broadcasted_iota(jnp.int32, (KEY_TILE, nl), 0)
            s = jnp.where(k_pos <= q_pos, s, NEG_INF)
        m_new = jnp.maximum(m_run, jnp.max(s, axis=0, keepdims=True))
        m_safe = jnp.where(m_new == NEG_INF, 0.0, m_new)
        alpha = jnp.exp(m_run - m_safe)
        p = jnp.exp(s - m_safe)
        l_new = alpha * l_run + jnp.sum(p, axis=0, keepdims=True)
        p_b = p.astype(BF16)
        acc_new = []
        for g in range(N_KV):
            pv = _dot(vst_ref[g * hd:(g + 1) * hd, pl.ds(off, KEY_TILE)], p_b[:, g * gl:(g + 1) * gl])
            acc_new.append(alpha[:, g * gl:(g + 1) * gl] * acc[g] + pv)
        return m_new, l_new, tuple(acc_new)

    n_full = (start + nq - 1) // KEY_TILE
    init = (jnp.full((1, nl), NEG_INF, F32), jnp.zeros((1, nl), F32),
            tuple(jnp.zeros((hd, gl), F32) for _ in range(N_KV)))
    carry = lax.fori_loop(0, n_full, lambda j, c: tile(j, c, False), init)
    m_run, l_run, acc = tile(n_full, carry, True)
    inv_l = 1.0 / l_run
    for g in range(N_KV):
        o_s.append(acc[g] * inv_l[:, g * gl:(g + 1) * gl])

    span = WINDOW + nq
    w0 = pl.multiple_of(jnp.maximum(start - WINDOW, 0), nq)
    s_w = _dot(kw_ref[0, pl.ds(w0, span), :], qbd)
    dpos = q_pos - (w0 + lax.broadcasted_iota(jnp.int32, (span, nl), 0))
    p_w = _masked_softmax_cols(s_w, (dpos >= 0) & (dpos < WINDOW)).astype(BF16)
    for g in range(N_KV):
        o_w.append(_dot(vwt_ref[g * hd:(g + 1) * hd, pl.ds(w0, span)], p_w[:, g * gl:(g + 1) * gl]))

    gates = _sigmoid(gt_ref[...])
    for g in range(N_KV):
        for r in range(N_REP):
            y = jnp.zeros((hd, nq), F32)
            for br, o in enumerate((o_c, o_s, o_w)):
                row = 2 * M_HEADS + br * N_HEADS + g * N_REP + r
                y = y + gates[row:row + 1, :] * o[g][:, r * nq:(r + 1) * nq]
            out_ref[(g * N_REP + r) * hd:(g * N_REP + r + 1) * hd, :] = y


def _nsa_prompt(qt, smallt, ck, cvt, ks, vst, kw, vwt, *, bsz, t):
    nw = qt.shape[0]
    hd = nw // N_HEADS
    nb = ck.shape[1]
    nqb = t // Q_BLOCK
    tok = lambda rows: pl.BlockSpec((rows, Q_BLOCK), lambda b, i: (0, b * nqb + i))
    seq_rows = lambda a: pl.BlockSpec((1,) + a.shape[1:], lambda b, i: (b, 0, 0))
    seq_cols = lambda a: pl.BlockSpec((a.shape[0], t), lambda b, i: (0, b))
    return pl.pallas_call(
        functools.partial(_nsa_prompt_kernel, hd=hd, nb=nb),
        grid=(bsz, nqb),
        in_specs=[tok(nw), tok(32), seq_rows(ck), seq_rows(cvt),
                  seq_rows(ks), seq_cols(vst), seq_rows(kw), seq_cols(vwt)],
        out_specs=tok(nw),
        out_shape=jax.ShapeDtypeStruct((nw, bsz * t), F32),
        scratch_shapes=[pltpu.VMEM((nb, N_HEADS * Q_BLOCK), F32)],
        compiler_params=_cparams("arbitrary", "arbitrary"), name="nsa_prompt",
    )(qt, smallt, ck, cvt, ks, vst, kw, vwt)


def _outproj_kernel(x_ref, h_ref, oz_ref, yn_ref, zn_ref, w_ref, g_ref, out_ref, *, mw, transposed):
    o_m = oz_ref[:, :mw]
    z_m = oz_ref[:, mw:]
    y_m = (_sigmoid(o_m) * h_ref[...]) * (z_m * _sigmoid(z_m))
    y_n = yn_ref[...].T if transposed else yn_ref[...]
    z_n = zn_ref[...]
    y_n = y_n * (z_n * _sigmoid(z_n))
    y = _dot(y_m.astype(BF16), w_ref[:mw, :]) + _dot(y_n.astype(BF16), w_ref[mw:, :])
    ms = jnp.mean(y * y, axis=-1, keepdims=True)
    out_ref[...] = x_ref[...] + (y * lax.rsqrt(ms + RMS_EPS)) * g_ref[...]


def _outproj(x2d, h_m, oz, yn, zn, w_out, g_post, *, transposed, tm):
    m, d = x2d.shape
    mw = h_m.shape[1]
    nw = zn.shape[1]
    row = lambda width: pl.BlockSpec((tm, width), lambda i: (i, 0))
    yn_spec = pl.BlockSpec((nw, tm), lambda i: (0, i)) if transposed else row(nw)
    w = w_out.astype(BF16)
    g2 = g_post[None, :]
    return pl.pallas_call(
        functools.partial(_outproj_kernel, mw=mw, transposed=transposed),
        grid=(m // tm,),
        in_specs=[row(d), row(mw), row(2 * mw), yn_spec, row(nw),
                  pl.BlockSpec(w.shape, lambda i: (0, 0)), pl.BlockSpec(g2.shape, lambda i: (0, 0))],
        out_specs=row(d), out_shape=jax.ShapeDtypeStruct((m, d), F32),
        compiler_params=_cparams("arbitrary"), name="outproj",
    )(x2d, h_m, oz, yn, zn, w, g2)


def _nsa_decode_kernel(table_ref, qbd_ref, gate_ref, ckv_ref, pages_ref, news_ref, win_ref, neww_ref,
                       out_ref, swin_ref, buf_ref, sem_ref, bias_ref, *, hd, n_pages, page, t_new, past, nbp):
    b = pl.program_id(0)
    nseq = pl.num_programs(0)
    slot = lax.rem(b, 2)
    fetch = functools.partial(_start_pages, table_ref, pages_ref, buf_ref, sem_ref,
                              n_pages=n_pages, page=page)

    @pl.when(b == 0)
    def _():
        fetch(seq=b, slot=slot)

    @pl.when(b + 1 < nseq)
    def _():
        fetch(seq=b + 1, slot=1 - slot)

    kw2 = N_KV * hd
    qbd = qbd_ref[0]
    lane = lax.broadcasted_iota(jnp.int32, (1, LANES), 1)
    q_pos = past + (lane & (t_new - 1))
    n_new_blocks = (past + t_new + BLOCK - 1) // BLOCK

    ckv = ckv_ref[0]
    s_c = _dot(ckv[:, :kw2].astype(BF16), qbd)
    nid = lax.broadcasted_iota(jnp.int32, (nbp, LANES), 0)
    p_c = _masked_softmax_cols(s_c, ((nid + 1) * BLOCK - 1 <= q_pos) & (nid < n_new_blocks))
    o_c = _dot_tn(ckv[:, kw2:].astype(BF16), p_c.astype(BF16))
    gl = N_REP * t_new
    imp = p_c
    for r in range(1, N_REP):
        imp = imp + pltpu.roll(p_c, LANES - r * t_new, 1)
    sel = _select_blocks(imp, q_pos, n_new_blocks)
    sel = jnp.where(((lane & (gl - 1)) < t_new) & (lane < N_KV * gl), sel, 0.0)
    sel_all = sel
    for r in range(1, N_REP):
        sel_all = sel_all + pltpu.roll(sel, r * t_new, 1)
    bias_ref[...] = jnp.where(sel_all > 0, 0.0, NEG_INF)

    _wait_pages(table_ref, pages_ref, buf_ref, sem_ref, b, slot, n_pages, page)

    blocks_per_tile = KEY_TILE // BLOCK
    n_tiles = past // KEY_TILE

    def tile(j, carry):
        m_run, l_run, acc = carry
        off = pl.multiple_of(j * KEY_TILE, KEY_TILE)
        s = _dot(buf_ref[slot, 0, pl.ds(off, KEY_TILE), :].astype(BF16), qbd)
        boff = pl.multiple_of(j * blocks_per_tile, blocks_per_tile)
        bias = bias_ref[pl.ds(boff, blocks_per_tile), :]
        parts = [s[n * BLOCK:(n + 1) * BLOCK, :] + bias[n:n + 1, :] for n in range(blocks_per_tile)]
        s = jnp.concatenate(parts, axis=0)
        m_new = jnp.maximum(m_run, jnp.max(s, axis=0, keepdims=True))
        m_safe = jnp.where(m_new == NEG_INF, 0.0, m_new)
        alpha = jnp.exp(m_run - m_safe)
        p = jnp.exp(s - m_safe)
        l_new = alpha * l_run + jnp.sum(p, axis=0, keepdims=True)
        acc = alpha * acc + _dot_tn(buf_ref[slot, 1, pl.ds(off, KEY_TILE), :].astype(BF16), p.astype(BF16))
        return m_new, l_new, acc

    init = (jnp.full((1, LANES), NEG_INF, F32), jnp.zeros((1, LANES), F32), jnp.zeros((kw2, LANES), F32))
    m_run, l_run, acc = lax.fori_loop(0, n_tiles, tile, init)
    new_rows = news_ref[0]
    s = _dot(new_rows[:, :kw2].astype(BF16), qbd)
    k_pos = past + lax.broadcasted_iota(jnp.int32, (8, LANES), 0)
    ok = (k_pos <= q_pos) & (k_pos < past + t_new)
    s = jnp.where(ok, s + bias_ref[past // BLOCK:past // BLOCK + 1, :], NEG_INF)
    m_new = jnp.maximum(m_run, jnp.max(s, axis=0, keepdims=True))
    m_safe = jnp.where(m_new == NEG_INF, 0.0, m_new)
    alpha = jnp.exp(m_run - m_safe)
    p = jnp.exp(s - m_safe)
    l_run = alpha * l_run + jnp.sum(p, axis=0, keepdims=True)
    acc = alpha * acc + _dot_tn(new_rows[:, kw2:].astype(BF16), p.astype(BF16))
    o_s = acc / jnp.where(l_run > 0, l_run, 1.0)

    wb = win_ref.shape[1]
    win = win_ref[0]
    neww = neww_ref[0]
    s_old = _dot(win[:, :kw2].astype(BF16), qbd)
    s_new = _dot(neww[:, :kw2].astype(BF16), qbd)
    kp_old = past - wb + lax.broadcasted_iota(jnp.int32, (wb, LANES), 0)
    kp_new = past + lax.broadcasted_iota(jnp.int32, (8, LANES), 0)
    ok_old = (q_pos - kp_old >= 0) & (q_pos - kp_old < WINDOW) & (kp_old >= 0)
    ok_new = (q_pos - kp_new >= 0) & (q_pos - kp_new < WINDOW) & (kp_new < past + t_new)
    s_old = jnp.where(ok_old, s_old, NEG_INF)
    s_new = jnp.where(ok_new, s_new, NEG_INF)
    mx = jnp.maximum(jnp.max(s_old, axis=0, keepdims=True), jnp.max(s_new, axis=0, keepdims=True))
    mx = jnp.where(mx == NEG_INF, 0.0, mx)
    e_old = jnp.where(ok_old, jnp.exp(s_old - mx), 0.0)
    e_new = jnp.where(ok_new, jnp.exp(s_new - mx), 0.0)
    tot = jnp.sum(e_old, axis=0, keepdims=True) + jnp.sum(e_new, axis=0, keepdims=True)
    inv = 1.0 / jnp.where(tot > 0, tot, 1.0)
    o_w = (_dot_tn(win[:, kw2:].astype(BF16), (e_old * inv).astype(BF16))
           + _dot_tn(neww[:, kw2:].astype(BF16), (e_new * inv).astype(BF16)))

    gates = _sigmoid(gate_ref[0])
    out_ref[0] = gates[0:1, :] * o_c + gates[1:2, :] * o_s + gates[2:3, :] * o_w

    swin_ref[0, 0:wb - t_new, :] = win_ref[0, t_new:wb, :]
    swin_ref[0, wb - t_new:wb, :] = neww_ref[0, 0:t_new, :]


def _nsa_decode(table, qbd, gate, ckv, pages, news, win, neww, *, hd, t_new):
    s, n_pages = table.shape
    _, page, width = pages.shape
    past = n_pages * page
    nbp = ckv.shape[1]
    wb = win.shape[1]
    assert past % KEY_TILE == 0
    seq = lambda a: pl.BlockSpec((1,) + a.shape[1:], lambda b, t: (b, 0, 0))
    return pl.pallas_call(
        functools.partial(_nsa_decode_kernel, hd=hd, n_pages=n_pages, page=page, t_new=t_new,
                          past=past, nbp=nbp),
        grid_spec=pltpu.PrefetchScalarGridSpec(
            num_scalar_prefetch=1, grid=(s,),
            in_specs=[seq(qbd), seq(gate), seq(ckv), pl.BlockSpec(memory_space=pl.ANY),
                      seq(news), seq(win), seq(neww)],
            out_specs=(pl.BlockSpec((1, N_KV * hd, LANES), lambda b, t: (b, 0, 0)), seq(win)),
            scratch_shapes=[pltpu.VMEM((2, 2, past, LANES), F32), pltpu.SemaphoreType.DMA((2,)),
                            pltpu.VMEM((nbp, LANES), F32)]),
        out_shape=(jax.ShapeDtypeStruct((s, N_KV * hd, LANES), F32),
                   jax.ShapeDtypeStruct(win.shape, F32)),
        compiler_params=_cparams("arbitrary"), name="nsa_decode",
    )(table, qbd, gate, ckv, pages, news, win, neww)


MLSTM_CHUNK = 128
DECODE_PAD = 16
ROW_TILE = 256


def _pad_rows(a, n, value=0.0):
    return jnp.pad(a, ((0, 0), (0, n - a.shape[1])) + ((0, 0),) * (a.ndim - 2), constant_values=value)


def _prompt_layer(x, prm):
    g_pre, wn, wt, bn, bt, bd, w_out, g_post, dims = prm
    mw, nw, kvw = dims
    bsz, t, d = x.shape
    hd = nw // N_HEADS
    x2 = x.reshape(bsz * t, d)
    (qkv, oz, zn, kvc, kvs, kvwin, small, ks, kw, qt, vst, vwt, smallt) = _inproj(
        x2, g_pre, wn, wt, bn, bt, mw=mw, nw=nw, kvw=kvw, tm=ROW_TILE)
    dh = mw // M_HEADS
    gates_t = smallt[:8].reshape(8, bsz, t).transpose(1, 0, 2)
    c0 = jnp.zeros((bsz, M_HEADS, dh, dh), F32)
    n0 = jnp.zeros((bsz, M_HEADS, dh), F32)
    m0 = jnp.full((bsz, M_HEADS), NEG_INF, F32)
    h_m, c, n, m = _mlstm(qkv.reshape(bsz, t, 3 * mw), small.reshape(bsz, t, LANES), gates_t,
                          c0, n0, m0, chunk=math.gcd(t, MLSTM_CHUNK))
    page = 2 * BLOCK
    pages_per_step = math.gcd(t // page, 64)
    table = jnp.arange(bsz * t // page, dtype=jnp.int32).reshape(-1, pages_per_step)
    ckv = _compress(kvc.reshape(-1, page, kvw), table, bd).reshape(bsz, t // BLOCK, kvw)
    ck = ckv[:, :, :kvw // 2].astype(BF16)
    cvt = ckv[:, :, kvw // 2:].transpose(0, 2, 1).astype(BF16)
    ynt = _nsa_prompt(qt, smallt, ck, cvt, ks.reshape(bsz, t, kvw // 2), vst,
                      kw.reshape(bsz, t, kvw // 2), vwt, bsz=bsz, t=t)
    y = _outproj(x2, h_m.reshape(bsz * t, mw), oz, ynt, zn, w_out, g_post, transposed=True, tm=ROW_TILE)
    kv5 = lambda a: a.reshape(bsz, t, 2, N_KV, hd)
    wlen = min(WINDOW, t)
    return y.reshape(bsz, t, d), (kv5(kvc), kv5(kvs), kv5(kvwin)[:, t - wlen:], c, n, m)


def _sample_layer(x, cache_cmp, cache_sel, win_kv, c0, n0, m0, page_table, prm):
    g_pre, wn, wt, bn, bt, bd, w_out, g_post, dims = prm
    mw, nw, kvw = dims
    bsz, t, d = x.shape
    hd = nw // N_HEADS
    dh = mw // M_HEADS
    n_pages = page_table.shape[1]
    page = cache_cmp.shape[1]
    past = n_pages * page
    x2 = x.reshape(bsz * t, d)
    m_rows = bsz * t
    m_pad = -(-m_rows // ROW_TILE) * ROW_TILE
    x2p = jnp.pad(x2, ((0, m_pad - m_rows), (0, 0)))
    outs = _inproj(x2p, g_pre, wn, wt, bn, bt, mw=mw, nw=nw, kvw=kvw, tm=ROW_TILE)
    (qkv, oz, zn, kvc, kvs, kvwin, small) = [a[:m_rows] for a in outs[:7]]
    qt = outs[9][:, :m_rows]
    tp = DECODE_PAD
    small3 = small.reshape(bsz, t, LANES)
    pad_gate = jnp.concatenate([jnp.full((2 * M_HEADS // 2,), NEG_INF, F32),
                                jnp.full((2 * M_HEADS // 2,), jnp.inf, F32),
                                jnp.zeros((LANES - 2 * M_HEADS,), F32)])
    small_p = jnp.concatenate([small3, jnp.broadcast_to(pad_gate, (bsz, tp - t, LANES))], axis=1)
    gates_t = small_p[:, :, :8].transpose(0, 2, 1)
    h_m, c, n, m = _mlstm(_pad_rows(qkv.reshape(bsz, t, 3 * mw), tp), small_p, gates_t,
                          c0.astype(F32), n0.astype(F32), m0.astype(F32), chunk=tp)
    h_m = h_m[:, :t]
    ckv_old = _compress(cache_cmp.reshape(-1, page, kvw), page_table, bd)
    new_blocks = -(-(past + t) // BLOCK) - past // BLOCK
    assert past % BLOCK == 0 and new_blocks * BLOCK <= page
    new_page = _pad_rows(kvc.reshape(bsz, t, kvw), page)
    new_table = jnp.arange(bsz, dtype=jnp.int32).reshape(-1, math.gcd(bsz, 64))
    ckv_new = _compress(new_page, new_table, bd).reshape(bsz, page // BLOCK, kvw)[:, :new_blocks]
    nbp = -(-(past // BLOCK + new_blocks) // 8) * 8
    ckv = _pad_rows(jnp.concatenate([ckv_old, ckv_new], axis=1), nbp)
    qd = qt.reshape(N_KV, N_REP, hd, bsz, t).transpose(3, 0, 2, 1, 4)
    qbd = jnp.zeros((bsz, N_KV, hd, N_KV, N_REP, t), BF16)
    for g in range(N_KV):
        qbd = qbd.at[:, g, :, g].set(qd[:, g])
    qbd = qbd.reshape(bsz, N_KV * hd, N_KV * N_REP * t)
    qbd = jnp.pad(qbd, ((0, 0), (0, 0), (0, LANES - qbd.shape[2])))
    g_n = small3[:, :, 2 * M_HEADS:2 * M_HEADS + 3 * N_HEADS].reshape(bsz, t, 3, N_KV * N_REP)
    gate = g_n.transpose(0, 2, 3, 1).reshape(bsz, 3, N_KV * N_REP * t)
    gate = jnp.pad(gate, ((0, 0), (0, 8 - 3), (0, LANES - gate.shape[2])))
    o_t, s_win = _nsa_decode(page_table, qbd, gate, ckv, cache_sel.reshape(-1, page, kvw),
                             _pad_rows(kvs.reshape(bsz, t, kvw), 8), win_kv.reshape(bsz, -1, kvw),
                             _pad_rows(kvwin.reshape(bsz, t, kvw), 8), hd=hd, t_new=t)
    o6 = o_t[:, :, :N_KV * N_REP * t].reshape(bsz, N_KV, hd, N_KV, N_REP, t)
    yn = jnp.stack([o6[:, g, :, g] for g in range(N_KV)], axis=1)
    yn = yn.transpose(0, 4, 1, 3, 2).reshape(m_rows, nw)
    pad2 = lambda a: jnp.pad(a, ((0, m_pad - m_rows), (0, 0)))
    y = _outproj(x2p, pad2(h_m.reshape(m_rows, mw)), pad2(oz), pad2(yn), pad2(zn), w_out, g_post,
                 transposed=False, tm=ROW_TILE)[:m_rows]
    kv5 = lambda a: a.reshape(bsz, t, 2, N_KV, hd)
    return y.reshape(bsz, t, d), (kv5(kvc), kv5(kvs), s_win.reshape(win_kv.shape), c, n, m)


def kernel(x_prompt, x_sample, cache_cmp_kv, cache_sel_kv, state_win_kv, state_mlstm_c, state_mlstm_n,
           state_mlstm_m, page_table, g_pre, w_in, b_i, b_f, w_cmp_k, w_cmp_v, w_out, g_post):
    depth = w_in.shape[0]
    d = x_prompt.shape[-1]
    hd = w_cmp_k.shape[-1]
    nw = N_HEADS * hd
    mw = w_out.shape[1] - nw
    kvw = 2 * N_KV * hd
    xp, xs = x_prompt, x_sample
    new_p = [[] for _ in range(6)]
    new_s = [[] for _ in range(6)]
    for layer in range(depth):
        wn, wt, bn, bt = _split_weights(w_in[layer], b_i[layer], b_f[layer], mw, nw, kvw)
        bd = _compress_weights(w_cmp_k[layer], w_cmp_v[layer])
        prm = (g_pre[layer], wn, wt, bn, bt, bd, w_out[layer], g_post[layer], (mw, nw, kvw))
        xp, st_p = _prompt_layer(xp, prm)
        n_phys, page = cache_cmp_kv.shape[1:3]
        xs, st_s = _sample_layer(xs, cache_cmp_kv[layer].reshape(n_phys, page, kvw),
                                 cache_sel_kv[layer].reshape(n_phys, page, kvw),
                                 state_win_kv[layer], state_mlstm_c[layer], state_mlstm_n[layer],
                                 state_mlstm_m[layer], page_table, prm)
        for lst, a in zip(new_p, st_p):
            lst.append(a)
        for lst, a in zip(new_s, st_s):
            lst.append(a)
    return (xp, xs) + tuple(jnp.stack(a) for a in new_p) + tuple(jnp.stack(a) for a in new_s)
```

```python
import functools
import math

import jax
import jax.numpy as jnp
from jax import lax
from jax.experimental import pallas as pl
from jax.experimental.pallas import tpu as pltpu

M_HEADS = 4
N_HEADS = 8
N_KV = 2
N_REP = N_HEADS // N_KV
BLOCK = 64
N_SELECT = 16
WINDOW = 512
Q_BLOCK = 128
RMS_EPS = 1e-6

LANES = 128
F32 = jnp.float32
BF16 = jnp.bfloat16
NEG_INF = float("-inf")
LOG2E = math.log2(math.e)
MASK_NEG = -1e30

VMEM_LIMIT_BYTES = 56 * 1024 * 1024


def _cparams(*sem):
    return pltpu.CompilerParams(dimension_semantics=sem, vmem_limit_bytes=VMEM_LIMIT_BYTES)


def _dot(a, b):
    return jnp.dot(a, b, preferred_element_type=F32)


def _dot_nt(a, b):
    return lax.dot_general(a, b, (((1,), (1,)), ((), ())), preferred_element_type=F32)


def _dot_tn(a, b):
    return lax.dot_general(a, b, (((0,), (0,)), ((), ())), preferred_element_type=F32)


def _sigmoid(x):
    return 1.0 / (1.0 + jnp.exp(-x))


def _log_sigmoid(x):
    return jnp.minimum(x, 0.0) - jnp.log1p(jnp.exp(-jnp.abs(x)))


def _inproj_kernel(x_ref, g_ref, wn_ref, wt_ref, bn_ref, bt_ref,
                   qkv_ref, oz_ref, zn_ref, kvc_ref, kvs_ref, kvw_ref, small_ref,
                   ks_ref, kw_ref, qt_ref, vst_ref, vwt_ref, smallt_ref, *, mw, nw, kvw, scale):
    x = x_ref[...]
    ms = jnp.mean(x * x, axis=-1, keepdims=True)
    xn = ((x * lax.rsqrt(ms + RMS_EPS)) * g_ref[...]).astype(BF16)
    c = 0
    qkv_ref[...] = _dot(xn, wn_ref[:, c:c + 3 * mw])
    c += 3 * mw
    oz_ref[...] = _dot(xn, wn_ref[:, c:c + 2 * mw])
    c += 2 * mw
    zn_ref[...] = _dot(xn, wn_ref[:, c:c + nw])
    c += nw
    kvc_ref[...] = _dot(xn, wn_ref[:, c:c + kvw])
    c += kvw
    kvs = _dot(xn, wn_ref[:, c:c + kvw])
    kvs_ref[...] = kvs
    ks_ref[...] = kvs[:, :kvw // 2].astype(BF16)
    c += kvw
    kvwin = _dot(xn, wn_ref[:, c:c + kvw])
    kvw_ref[...] = kvwin
    kw_ref[...] = kvwin[:, :kvw // 2].astype(BF16)
    c += kvw
    small_ref[...] = _dot(xn, wn_ref[:, c:c + LANES]) + bn_ref[...]
    r = 0
    qt_ref[...] = (_dot_nt(wt_ref[r:r + nw, :], xn) * scale).astype(BF16)
    r += nw
    vst_ref[...] = _dot_nt(wt_ref[r:r + kvw // 2, :], xn).astype(BF16)
    r += kvw // 2
    vwt_ref[...] = _dot_nt(wt_ref[r:r + kvw // 2, :], xn).astype(BF16)
    r += kvw // 2
    smallt_ref[...] = _dot_nt(wt_ref[r:r + 32, :], xn) + bt_ref[...]


def _split_weights(w_in, b_i, b_f, mw, nw, kvw):
    d = w_in.shape[0]
    o = 0
    qkv_m = w_in[:, o:o + 3 * mw]; o += 3 * mw
    oz_m = w_in[:, o:o + 2 * mw]; o += 2 * mw
    w_i = w_in[:, o:o + M_HEADS]; o += M_HEADS
    w_f = w_in[:, o:o + M_HEADS]; o += M_HEADS
    q_n = w_in[:, o:o + nw]; o += nw
    kv_c = w_in[:, o:o + kvw]; o += kvw
    kv_s = w_in[:, o:o + kvw]; o += kvw
    kv_w = w_in[:, o:o + kvw]; o += kvw
    g_n = w_in[:, o:o + 3 * N_HEADS]; o += 3 * N_HEADS
    z_n = w_in[:, o:o + nw]; o += nw
    assert o == w_in.shape[1]
    n_small = 2 * M_HEADS + 3 * N_HEADS
    small = jnp.concatenate([w_i, w_f, g_n, jnp.zeros((d, LANES - n_small), w_in.dtype)], axis=1)
    wn = jnp.concatenate([qkv_m, oz_m, z_n, kv_c, kv_s, kv_w, small], axis=1).astype(BF16)
    wt = jnp.concatenate([q_n, kv_s[:, kvw // 2:], kv_w[:, kvw // 2:], small[:, :32]], axis=1).T.astype(BF16)
    bias = jnp.concatenate([b_i, b_f, jnp.zeros((LANES - 2 * M_HEADS,), F32)])
    return wn, wt, bias[None, :], bias[:32, None]


def _inproj(x2d, g_pre, wn, wt, bn, bt, *, mw, nw, kvw, tm):
    m, d = x2d.shape
    assert m % tm == 0 and tm % LANES == 0
    grid = (m // tm,)
    row = lambda width: pl.BlockSpec((tm, width), lambda i: (i, 0))
    col = lambda height: pl.BlockSpec((height, tm), lambda i: (0, i))
    full = lambda a: pl.BlockSpec(a.shape, lambda i: (0,) * a.ndim)
    g2 = g_pre[None, :]
    out_shape = (
        jax.ShapeDtypeStruct((m, 3 * mw), F32), jax.ShapeDtypeStruct((m, 2 * mw), F32),
        jax.ShapeDtypeStruct((m, nw), F32),
        jax.ShapeDtypeStruct((m, kvw), F32), jax.ShapeDtypeStruct((m, kvw), F32),
        jax.ShapeDtypeStruct((m, kvw), F32), jax.ShapeDtypeStruct((m, LANES), F32),
        jax.ShapeDtypeStruct((m, kvw // 2), BF16), jax.ShapeDtypeStruct((m, kvw // 2), BF16),
        jax.ShapeDtypeStruct((nw, m), BF16), jax.ShapeDtypeStruct((kvw // 2, m), BF16),
        jax.ShapeDtypeStruct((kvw // 2, m), BF16), jax.ShapeDtypeStruct((32, m), F32),
    )
    out_specs = (row(3 * mw), row(2 * mw), row(nw), row(kvw), row(kvw), row(kvw), row(LANES),
                 row(kvw // 2), row(kvw // 2), col(nw), col(kvw // 2), col(kvw // 2), col(32))
    hd = nw // N_HEADS
    return pl.pallas_call(
        functools.partial(_inproj_kernel, mw=mw, nw=nw, kvw=kvw, scale=LOG2E * hd ** -0.5),
        grid=grid,
        in_specs=[row(d), full(g2), full(wn), full(wt), full(bn), full(bt)],
        out_specs=out_specs, out_shape=out_shape,
        compiler_params=_cparams("arbitrary"), name="inproj",
    )(x2d, g2, wn, wt, bn, bt)


def _mlstm_kernel(qkv_ref, small_ref, smallt_ref, c0_ref, n0_ref, m0_ref,
                  h_ref, c_ref, n_ref, m_ref, c_s, n_s, m_s, *, chunk, dh):
    ci = pl.program_id(1)
    nh = M_HEADS
    mw = nh * dh

    @pl.when(ci == 0)
    def _():
        c_s[...] = c0_ref[0]
        n_s[...] = n0_ref[0]
        m_s[...] = m0_ref[0]

    t_idx = lax.broadcasted_iota(jnp.int32, (chunk, chunk), 0)
    s_idx = lax.broadcasted_iota(jnp.int32, (chunk, chunk), 1)
    causal = s_idx <= t_idx
    scale = dh ** -0.5
    for h in range(nh):
        q = qkv_ref[0, :, h * dh:(h + 1) * dh]
        k = qkv_ref[0, :, mw + h * dh:mw + (h + 1) * dh] * scale
        v = qkv_ref[0, :, 2 * mw + h * dh:2 * mw + (h + 1) * dh]
        ig_c = small_ref[0, :, h:h + 1]
        lf_c = _log_sigmoid(small_ref[0, :, nh + h:nh + h + 1])
        ig_r = smallt_ref[0, h:h + 1, :]
        lf_r = _log_sigmoid(smallt_ref[0, nh + h:nh + h + 1, :])
        b_c = jnp.sum(jnp.where(causal, lf_r, 0.0), axis=1, keepdims=True)
        b_r = jnp.sum(jnp.where(t_idx <= s_idx, lf_c, 0.0), axis=0, keepdims=True)
        m_prev = m_s[h:h + 1, 0:1]
        n_prev = n_s[h:h + 1, :]
        c_prev = c_s[h]
        inter = b_c + m_prev
        dmat = jnp.where(causal, (b_c - b_r) + ig_r, NEG_INF)
        mt = jnp.maximum(inter, jnp.max(dmat, axis=1, keepdims=True))
        w_d = jnp.exp(dmat - mt)
        w_i = jnp.exp(inter - mt)
        q_b = q.astype(BF16)
        k_b = k.astype(BF16)
        qk = _dot_nt(q_b, k_b) * w_d
        num = w_i * _dot_nt(q_b, c_prev.astype(BF16)) + _dot(qk.astype(BF16), v.astype(BF16))
        den = w_i * jnp.sum(q * n_prev, axis=1, keepdims=True) + jnp.sum(qk, axis=1, keepdims=True)
        h_ref[0, :, h * dh:(h + 1) * dh] = num / jnp.maximum(jnp.abs(den), jnp.exp(-mt))
        b_last = b_c[chunk - 1:chunk, :]
        m_last = mt[chunk - 1:chunk, :]
        wi_last = w_i[chunk - 1:chunk, :]
        w_last = jnp.exp(((b_last - b_c) + ig_c) - m_last)
        c_s[h] = wi_last * c_prev + _dot_tn((v * w_last).astype(BF16), k_b)
        n_s[h:h + 1, :] = wi_last * n_prev + jnp.sum(k * w_last, axis=0, keepdims=True)
        m_s[h:h + 1, :] = jnp.broadcast_to(m_last, (1, LANES))

    @pl.when(ci == pl.num_programs(1) - 1)
    def _():
        c_ref[0] = c_s[...]
        n_ref[0] = n_s[...]
        m_ref[0] = m_s[...]


def _mlstm(qkv, small, smallt, c0, n0, m0, *, chunk):
    s, t, mw3 = qkv.shape
    mw = mw3 // 3
    dh = mw // M_HEADS
    assert t % chunk == 0
    nc = t // chunk
    m0b = jnp.broadcast_to(m0[:, :, None], (s, M_HEADS, LANES))
    state = lambda *tail: pl.BlockSpec((1,) + tail, lambda b, c: (b,) + (0,) * len(tail))
    h, c, n, m = pl.pallas_call(
        functools.partial(_mlstm_kernel, chunk=chunk, dh=dh),
        grid=(s, nc),
        in_specs=[pl.BlockSpec((1, chunk, mw3), lambda b, c: (b, c, 0)),
                  pl.BlockSpec((1, chunk, LANES), lambda b, c: (b, c, 0)),
                  pl.BlockSpec((1, 8, chunk), lambda b, c: (b, 0, c)),
                  state(M_HEADS, dh, dh), state(M_HEADS, dh), state(M_HEADS, LANES)],
        out_specs=(pl.BlockSpec((1, chunk, mw), lambda b, c: (b, c, 0)),
                   state(M_HEADS, dh, dh), state(M_HEADS, dh), state(M_HEADS, LANES)),
        out_shape=(jax.ShapeDtypeStruct((s, t, mw), F32),
                   jax.ShapeDtypeStruct((s, M_HEADS, dh, dh), F32),
                   jax.ShapeDtypeStruct((s, M_HEADS, dh), F32),
                   jax.ShapeDtypeStruct((s, M_HEADS, LANES), F32)),
        scratch_shapes=[pltpu.VMEM((M_HEADS, dh, dh), F32), pltpu.VMEM((M_HEADS, dh), F32),
                        pltpu.VMEM((M_HEADS, LANES), F32)],
        compiler_params=_cparams("arbitrary", "arbitrary"), name="mlstm",
    )(qkv, small, smallt, c0, n0, m0b)
    return h, c, n, m[:, :, 0]


PAGE = 2 * BLOCK
PITCH_PAD = 8


def _page_copy(table_ref, pages_ref, buf_ref, sem_ref, seq, slot, p, rows, pitch):
    return pltpu.make_async_copy(pages_ref.at[table_ref[seq, p]],
                                 buf_ref.at[slot, pl.ds(pl.multiple_of(p * pitch, 8), rows), :],
                                 sem_ref.at[slot])


def _start_pages(table_ref, pages_ref, buf_ref, sem_ref, seq, slot, n_pages, rows, pitch):
    def body(p, carry):
        _page_copy(table_ref, pages_ref, buf_ref, sem_ref, seq, slot, p, rows, pitch).start()
        return carry
    lax.fori_loop(0, n_pages, body, 0)


def _wait_pages(table_ref, pages_ref, buf_ref, sem_ref, seq, slot, n_pages, rows, pitch):
    def body(p, carry):
        _page_copy(table_ref, pages_ref, buf_ref, sem_ref, seq, slot, p, rows, pitch).wait()
        return carry
    lax.fori_loop(0, n_pages, body, 0)


def _prefetch_pages(table_ref, pages_ref, buf_ref, sem_ref, n_pages, rows, pitch):
    b = pl.program_id(0)
    slot = lax.rem(b, 2)
    fetch = functools.partial(_start_pages, table_ref, pages_ref, buf_ref, sem_ref,
                              n_pages=n_pages, rows=rows, pitch=pitch)

    @pl.when(b == 0)
    def _():
        fetch(seq=b, slot=slot)

    @pl.when(b + 1 < pl.num_programs(0))
    def _():
        fetch(seq=b + 1, slot=1 - slot)

    return b, slot


def _compress_kernel(table_ref, pages_ref, bd_ref, out_ref, buf_ref, sem_ref, *, n_pages, rows, pitch, hd):
    b, slot = _prefetch_pages(table_ref, pages_ref, buf_ref, sem_ref, n_pages, rows, pitch)
    _wait_pages(table_ref, pages_ref, buf_ref, sem_ref, b, slot, n_pages, rows, pitch)

    def body(dp, acc):
        new = []
        for kv in range(2):
            parts = []
            for g in range(N_KV):
                r0 = (kv * N_KV + g) * hd + 2 * dp
                parts.append(jnp.concatenate(
                    [buf_ref[slot, pl.ds(r0 + j, n_pages, stride=pitch), :] for j in range(2)], axis=1))
            lhs = jnp.concatenate(parts, axis=0).astype(BF16)
            new.append(acc[kv] + _dot(lhs, bd_ref[kv, dp]))
        return tuple(new)

    acc = lax.fori_loop(0, hd // 2, body, tuple(jnp.zeros((N_KV * n_pages, LANES), F32) for _ in range(2)),
                        unroll=4)
    for kv in range(2):
        out_ref[0, kv] = acc[kv]


def _compress_weights(w_ck, w_cv):
    hd = w_ck.shape[1]
    nblk = PAGE // BLOCK
    planes = []
    for w in (w_ck, w_cv):
        wd = w.reshape(BLOCK, hd // 2, 2, hd).transpose(1, 2, 0, 3)
        z = jnp.zeros((hd // 2, 2, nblk, BLOCK, nblk, hd), w.dtype)
        for n in range(nblk):
            z = z.at[:, :, n, :, n, :].set(wd)
        planes.append(z.reshape(hd // 2, 2 * PAGE, nblk * hd))
    return jnp.stack(planes).astype(BF16)


def _compress(pages, table, bd):
    _, rows, page = pages.shape
    assert page == PAGE == LANES
    s, n_pages = table.shape
    hd = rows // (2 * N_KV)
    pitch = rows + PITCH_PAD
    nblk = PAGE // BLOCK
    out = pl.pallas_call(
        functools.partial(_compress_kernel, n_pages=n_pages, rows=rows, pitch=pitch, hd=hd),
        grid_spec=pltpu.PrefetchScalarGridSpec(
            num_scalar_prefetch=1, grid=(s,),
            in_specs=[pl.BlockSpec(memory_space=pl.ANY),
                      pl.BlockSpec(bd.shape, lambda b, t: (0, 0, 0, 0))],
            out_specs=pl.BlockSpec((1, 2, N_KV * n_pages, LANES), lambda b, t: (b, 0, 0, 0)),
            scratch_shapes=[pltpu.VMEM((2, n_pages * pitch, LANES), F32),
                            pltpu.SemaphoreType.DMA((2,))]),
        out_shape=jax.ShapeDtypeStruct((s, 2, N_KV * n_pages, LANES), F32),
        compiler_params=_cparams("arbitrary"), name="compress",
    )(table, pages, bd)
    out = out.reshape(s, 2, N_KV, n_pages, nblk, hd).transpose(0, 3, 4, 1, 2, 5)
    return out.reshape(s, n_pages * nblk, 2 * N_KV * hd)


def _masked_softmax_cols(s, mask):
    s = jnp.where(mask, s, NEG_INF)
    mx = jnp.max(s, axis=0, keepdims=True)
    mx = jnp.where(mx == NEG_INF, 0.0, mx)
    e = jnp.where(mask, jnp.exp2(s - mx), 0.0)
    tot = jnp.sum(e, axis=0, keepdims=True)
    return e / jnp.where(tot > 0, tot, 1.0)


def _select_blocks(imp, q_pos, n_valid):
    nb, nq = imp.shape
    nid = lax.broadcasted_iota(jnp.int32, (nb, nq), 0)
    cur = jnp.right_shift(q_pos, BLOCK.bit_length() - 1)
    complete = (nid + 1) * BLOCK - 1 <= q_pos
    forced = ((nid == 0) | (nid == cur) | (nid == cur - 1)) & (nid <= cur)
    score = jnp.where(forced, jnp.inf, jnp.where(complete, imp, NEG_INF))
    score = jnp.where(nid < n_valid, score, NEG_INF)

    def body(_, carry):
        score, sel = carry
        mx = jnp.max(score, axis=0, keepdims=True)
        first = jnp.min(jnp.where(score == mx, nid, nb), axis=0, keepdims=True)
        pick = nid == first
        sel = jnp.where(pick & (mx > NEG_INF), 1.0, sel)
        score = jnp.where(pick, NEG_INF, score)
        return score, sel

    _, sel = lax.fori_loop(0, min(N_SELECT, n_valid), body, (score, jnp.zeros((nb, nq), F32)))
    return sel


def _block_diag_queries(qt, hd):
    nq = qt.shape[1]
    z = jnp.zeros((hd, nq), qt.dtype)
    rows = []
    for g in range(N_KV):
        parts = []
        for gg in range(N_KV):
            for r in range(N_REP):
                parts.append(qt[(g * N_REP + r) * hd:(g * N_REP + r + 1) * hd, :] if gg == g else z)
        rows.append(jnp.concatenate(parts, axis=1))
    return jnp.concatenate(rows, axis=0)


KEY_TILE = 1024
SUM_ROWS = 16


def _nsa_prompt_kernel(qt_ref, gt_ref, ck_ref, cvt_ref, ks_ref, vst_ref, kw_ref, vwt_ref,
                       out_ref, sel_ref, rhs_ref, *, hd, nb):
    i = pl.program_id(1)
    nq = Q_BLOCK
    gl = N_REP * nq
    nl = N_KV * gl
    start = i * nq
    qbd = _block_diag_queries(qt_ref[...], hd)
    lane_q = lax.broadcasted_iota(jnp.int32, (1, nl), 1) & (nq - 1)
    q_pos = start + lane_q
    q_pos1 = start + lax.broadcasted_iota(jnp.int32, (1, nq), 1)

    s_c = _dot(ck_ref[0], qbd)
    nid = lax.broadcasted_iota(jnp.int32, (nb, nl), 0)
    p_c = _masked_softmax_cols(s_c, (nid + 1) * BLOCK - 1 <= q_pos)
    o_c, o_s, o_w = [], [], []
    for g in range(N_KV):
        pg = p_c[:, g * gl:(g + 1) * gl]
        o_c.append(_dot(cvt_ref[0, g * hd:(g + 1) * hd, :], pg.astype(BF16)))
        imp = pg[:, 0:nq]
        for r in range(1, N_REP):
            imp = imp + pg[:, r * nq:(r + 1) * nq]
        sel = _select_blocks(imp, q_pos1, nb)
        for r in range(N_REP):
            sel_ref[:, g * gl + r * nq:g * gl + (r + 1) * nq] = sel

    blocks_per_tile = KEY_TILE // BLOCK
    kw2 = N_KV * hd
    key_block = jnp.right_shift(lax.broadcasted_iota(jnp.int32, (KEY_TILE, kw2), 0), BLOCK.bit_length() - 1)
    onehot = jnp.where(key_block == lax.broadcasted_iota(jnp.int32, (KEY_TILE, kw2), 1), 1.0, 0.0).astype(BF16)
    ones_rows = jnp.ones((SUM_ROWS, KEY_TILE), BF16)
    rhs_ref[0:kw2, :] = qbd
    rhs_ref[kw2:2 * kw2, :] = jnp.zeros((kw2, nl), BF16)

    def tile(j, carry, diagonal):
        m_run, acc = carry
        off = pl.multiple_of(j * KEY_TILE, KEY_TILE)
        boff = pl.multiple_of(j * blocks_per_tile, blocks_per_tile)
        bias = jnp.where(sel_ref[pl.ds(boff, blocks_per_tile), :] > 0, 0.0, MASK_NEG)
        rhs_ref[kw2:kw2 + 2 * blocks_per_tile, :] = jnp.concatenate(
            [bias, jnp.zeros_like(bias)], axis=0).astype(BF16)
        lhs = jnp.concatenate([ks_ref[0, pl.ds(off, KEY_TILE), :], onehot], axis=1)
        s = _dot(lhs, rhs_ref[...])
        if diagonal:
            k_pos = off + lax.broadcasted_iota(jnp.int32, (KEY_TILE, nl), 0)
            s = jnp.where(k_pos <= q_pos, s, NEG_INF)
        m_new = jnp.maximum(m_run, jnp.max(s, axis=0, keepdims=True))
        m_safe = jnp.where(m_new < 0.5 * MASK_NEG, 0.0, m_new)
        alpha = jnp.exp2(m_run - m_safe)
        p_b = jnp.exp2(s - m_safe).astype(BF16)
        acc_new = []
        for g in range(N_KV):
            v_aug = jnp.concatenate([vst_ref[g * hd:(g + 1) * hd, pl.ds(off, KEY_TILE)], ones_rows], axis=0)
            pv = _dot(v_aug, p_b[:, g * gl:(g + 1) * gl])
            acc_new.append(alpha[:, g * gl:(g + 1) * gl] * acc[g] + pv)
        return m_new, tuple(acc_new)

    n_full = (start + nq - 1) // KEY_TILE
    init = (jnp.full((1, nl), NEG_INF, F32),
            tuple(jnp.zeros((hd + SUM_ROWS, gl), F32) for _ in range(N_KV)))
    carry = lax.fori_loop(0, n_full, lambda j, c: tile(j, c, False), init)
    m_run, acc = tile(n_full, carry, True)
    for g in range(N_KV):
        o_s.append(acc[g][:hd] * (1.0 / acc[g][hd:hd + 1]))

    span = WINDOW + nq
    w0 = pl.multiple_of(jnp.maximum(start - WINDOW, 0), nq)
    s_w = _dot(kw_ref[0, pl.ds(w0, span), :], qbd)
    dpos = q_pos - (w0 + lax.broadcasted_iota(jnp.int32, (span, nl), 0))
    p_w = _masked_softmax_cols(s_w, (dpos >= 0) & (dpos < WINDOW)).astype(BF16)
    for g in range(N_KV):
        o_w.append(_dot(vwt_ref[g * hd:(g + 1) * hd, pl.ds(w0, span)], p_w[:, g * gl:(g + 1) * gl]))

    gates = _sigmoid(gt_ref[...])
    for g in range(N_KV):
        for r in range(N_REP):
            y = jnp.zeros((hd, nq), F32)
            for br, o in enumerate((o_c, o_s, o_w)):
                row = 2 * M_HEADS + br * N_HEADS + g * N_REP + r
                y = y + gates[row:row + 1, :] * o[g][:, r * nq:(r + 1) * nq]
            out_ref[(g * N_REP + r) * hd:(g * N_REP + r + 1) * hd, :] = y


def _nsa_prompt(qt, smallt, ck, cvt, ks, vst, kw, vwt, *, bsz, t):
    nw = qt.shape[0]
    hd = nw // N_HEADS
    nb = ck.shape[1]
    nqb = t // Q_BLOCK
    tok = lambda rows: pl.BlockSpec((rows, Q_BLOCK), lambda b, i: (0, b * nqb + i))
    seq_rows = lambda a: pl.BlockSpec((1,) + a.shape[1:], lambda b, i: (b, 0, 0))
    seq_cols = lambda a: pl.BlockSpec((a.shape[0], t), lambda b, i: (0, b))
    return pl.pallas_call(
        functools.partial(_nsa_prompt_kernel, hd=hd, nb=nb),
        grid=(bsz, nqb),
        in_specs=[tok(nw), tok(32), seq_rows(ck), seq_rows(cvt),
                  seq_rows(ks), seq_cols(vst), seq_rows(kw), seq_cols(vwt)],
        out_specs=tok(nw),
        out_shape=jax.ShapeDtypeStruct((nw, bsz * t), F32),
        scratch_shapes=[pltpu.VMEM((nb, N_HEADS * Q_BLOCK), F32),
                        pltpu.VMEM((2 * N_KV * hd, N_HEADS * Q_BLOCK), BF16)],
        compiler_params=_cparams("arbitrary", "arbitrary"), name="nsa_prompt",
    )(qt, smallt, ck, cvt, ks, vst, kw, vwt)


def _outproj_kernel(x_ref, h_ref, oz_ref, yn_ref, zn_ref, w_ref, g_ref, out_ref, *, mw, transposed):
    o_m = oz_ref[:, :mw]
    z_m = oz_ref[:, mw:]
    y_m = (_sigmoid(o_m) * h_ref[...]) * (z_m * _sigmoid(z_m))
    y_n = yn_ref[...].T if transposed else yn_ref[...]
    z_n = zn_ref[...]
    y_n = y_n * (z_n * _sigmoid(z_n))
    y = _dot(y_m.astype(BF16), w_ref[:mw, :]) + _dot(y_n.astype(BF16), w_ref[mw:, :])
    ms = jnp.mean(y * y, axis=-1, keepdims=True)
    out_ref[...] = x_ref[...] + (y * lax.rsqrt(ms + RMS_EPS)) * g_ref[...]


def _outproj(x2d, h_m, oz, yn, zn, w_out, g_post, *, transposed, tm):
    m, d = x2d.shape
    mw = h_m.shape[1]
    nw = zn.shape[1]
    row = lambda width: pl.BlockSpec((tm, width), lambda i: (i, 0))
    yn_spec = pl.BlockSpec((nw, tm), lambda i: (0, i)) if transposed else row(nw)
    w = w_out.astype(BF16)
    g2 = g_post[None, :]
    return pl.pallas_call(
        functools.partial(_outproj_kernel, mw=mw, transposed=transposed),
        grid=(m // tm,),
        in_specs=[row(d), row(mw), row(2 * mw), yn_spec, row(nw),
                  pl.BlockSpec(w.shape, lambda i: (0, 0)), pl.BlockSpec(g2.shape, lambda i: (0, 0))],
        out_specs=row(d), out_shape=jax.ShapeDtypeStruct((m, d), F32),
        compiler_params=_cparams("arbitrary"), name="outproj",
    )(x2d, h_m, oz, yn, zn, w, g2)


DECODE_TILE = 512
PAGES_PER_TILE = DECODE_TILE // PAGE


def _nsa_decode_kernel(table_ref, qbd_ref, gate_ref, ckv_ref, pages_ref, news_ref, win_ref, neww_ref, newwt_ref,
                       out_ref, swin_ref, buf_ref, sem_ref, bias_ref, s_ref, *, hd, n_pages, t_new, past, nbp):
    kw2 = N_KV * hd
    rows = 2 * kw2
    b, slot = _prefetch_pages(table_ref, pages_ref, buf_ref, sem_ref, n_pages, rows, rows)
    qbd = qbd_ref[0]
    lane = lax.broadcasted_iota(jnp.int32, (1, LANES), 1)
    q_pos = past + (lane & (t_new - 1))
    n_new_blocks = (past + t_new + BLOCK - 1) // BLOCK

    ckv = ckv_ref[0]
    s_c = _dot(ckv[:, :kw2].astype(BF16), qbd)
    nid = lax.broadcasted_iota(jnp.int32, (nbp, LANES), 0)
    p_c = _masked_softmax_cols(s_c, ((nid + 1) * BLOCK - 1 <= q_pos) & (nid < n_new_blocks))
    o_c = _dot_tn(ckv[:, kw2:].astype(BF16), p_c.astype(BF16))
    gl = N_REP * t_new
    imp = p_c
    for r in range(1, N_REP):
        imp = imp + pltpu.roll(p_c, LANES - r * t_new, 1)
    sel = _select_blocks(imp, q_pos, n_new_blocks)
    sel = jnp.where(((lane & (gl - 1)) < t_new) & (lane < N_KV * gl), sel, 0.0)
    sel_all = sel
    for r in range(1, N_REP):
        sel_all = sel_all + pltpu.roll(sel, r * t_new, 1)
    bias_ref[...] = jnp.where(sel_all > 0, 0.0, NEG_INF)

    _wait_pages(table_ref, pages_ref, buf_ref, sem_ref, b, slot, n_pages, rows, rows)

    blocks_per_tile = DECODE_TILE // BLOCK
    blocks_per_page = PAGE // BLOCK
    n_tiles = past // DECODE_TILE

    def scores(j, m_run):
        boff = pl.multiple_of(j * blocks_per_tile, blocks_per_tile)
        bias = bias_ref[pl.ds(boff, blocks_per_tile), :]
        for q in range(PAGES_PER_TILE):
            p = j * PAGES_PER_TILE + q
            keys_t = buf_ref[slot, pl.ds(pl.multiple_of(p * rows, rows), kw2), :].astype(BF16)
            s = _dot_tn(keys_t, qbd)
            s = jnp.concatenate(
                [s[n * BLOCK:(n + 1) * BLOCK, :] + bias[q * blocks_per_page + n:q * blocks_per_page + n + 1, :]
                 for n in range(blocks_per_page)], axis=0)
            s_ref[pl.ds(pl.multiple_of(p * PAGE, PAGE), PAGE), :] = s
            m_run = jnp.maximum(m_run, jnp.max(s, axis=0, keepdims=True))
        return m_run

    m_run = lax.fori_loop(0, n_tiles, scores, jnp.full((1, LANES), NEG_INF, F32), unroll=4)
    new_rows = news_ref[0]
    s_new = _dot(new_rows[:, :kw2].astype(BF16), qbd)
    k_pos = past + lax.broadcasted_iota(jnp.int32, (8, LANES), 0)
    ok = (k_pos <= q_pos) & (k_pos < past + t_new)
    s_new = jnp.where(ok, s_new + bias_ref[past // BLOCK:past // BLOCK + 1, :], NEG_INF)
    m_all = jnp.maximum(m_run, jnp.max(s_new, axis=0, keepdims=True))
    m_safe = jnp.where(m_all == NEG_INF, 0.0, m_all)

    def values(j, carry):
        l_run, acc = carry
        for q in range(PAGES_PER_TILE):
            p = j * PAGES_PER_TILE + q
            pr = jnp.exp2(s_ref[pl.ds(pl.multiple_of(p * PAGE, PAGE), PAGE), :] - m_safe)
            l_run = l_run + jnp.sum(pr, axis=0, keepdims=True)
            vals_t = buf_ref[slot, pl.ds(pl.multiple_of(p * rows + kw2, kw2), kw2), :].astype(BF16)
            acc = acc + _dot(vals_t, pr.astype(BF16))
        return l_run, acc

    l_run, acc = lax.fori_loop(0, n_tiles, values, (jnp.zeros((1, LANES), F32), jnp.zeros((kw2, LANES), F32)),
                               unroll=4)
    p_new = jnp.exp2(s_new - m_safe)
    l_run = l_run + jnp.sum(p_new, axis=0, keepdims=True)
    acc = acc + _dot_tn(new_rows[:, kw2:].astype(BF16), p_new.astype(BF16))
    o_s = acc / jnp.where(l_run > 0, l_run, 1.0)

    wb = win_ref.shape[2]
    neww = neww_ref[0]
    s_old = _dot_tn(win_ref[0, 0:kw2, :].astype(BF16), qbd)
    s_new = _dot(neww[:, :kw2].astype(BF16), qbd)
    kp_old = past - wb + lax.broadcasted_iota(jnp.int32, (wb, LANES), 0)
    kp_new = past + lax.broadcasted_iota(jnp.int32, (8, LANES), 0)
    ok_old = (q_pos - kp_old >= 0) & (q_pos - kp_old < WINDOW) & (kp_old >= 0)
    ok_new = (q_pos - kp_new >= 0) & (q_pos - kp_new < WINDOW) & (kp_new < past + t_new)
    s_old = jnp.where(ok_old, s_old, NEG_INF)
    s_new = jnp.where(ok_new, s_new, NEG_INF)
    mx = jnp.maximum(jnp.max(s_old, axis=0, keepdims=True), jnp.max(s_new, axis=0, keepdims=True))
    mx = jnp.where(mx == NEG_INF, 0.0, mx)
    e_old = jnp.where(ok_old, jnp.exp2(s_old - mx), 0.0)
    e_new = jnp.where(ok_new, jnp.exp2(s_new - mx), 0.0)
    tot = jnp.sum(e_old, axis=0, keepdims=True) + jnp.sum(e_new, axis=0, keepdims=True)
    inv = 1.0 / jnp.where(tot > 0, tot, 1.0)
    o_w = (_dot(win_ref[0, kw2:rows, :].astype(BF16), (e_old * inv).astype(BF16))
           + _dot_tn(neww[:, kw2:].astype(BF16), (e_new * inv).astype(BF16)))

    gates = _sigmoid(gate_ref[0])
    out_ref[0] = gates[0:1, :] * o_c + gates[1:2, :] * o_s + gates[2:3, :] * o_w

    rolled = pltpu.roll(win_ref[0], wb - t_new, 1)
    lane_w = lax.broadcasted_iota(jnp.int32, (rows, LANES), 1)
    if wb > LANES:
        swin_ref[0, :, 0:wb - LANES] = rolled[:, 0:wb - LANES]
    swin_ref[0, :, wb - LANES:wb] = jnp.where(lane_w >= LANES - t_new, newwt_ref[0], rolled[:, wb - LANES:wb])


def _nsa_decode(table, qbd, gate, ckv, pages, news, win, neww, newwt, *, hd, t_new):
    s, n_pages = table.shape
    _, rows, page = pages.shape
    past = n_pages * page
    nbp = ckv.shape[1]
    wb = win.shape[2]
    assert page == PAGE and past % DECODE_TILE == 0 and wb % LANES == 0 and rows == 2 * N_KV * hd
    seq = lambda a: pl.BlockSpec((1,) + a.shape[1:], lambda b, t: (b, 0, 0))
    return pl.pallas_call(
        functools.partial(_nsa_decode_kernel, hd=hd, n_pages=n_pages, t_new=t_new, past=past, nbp=nbp),
        grid_spec=pltpu.PrefetchScalarGridSpec(
            num_scalar_prefetch=1, grid=(s,),
            in_specs=[seq(qbd), seq(gate), seq(ckv), pl.BlockSpec(memory_space=pl.ANY),
                      seq(news), seq(win), seq(neww), seq(newwt)],
            out_specs=(pl.BlockSpec((1, N_KV * hd, LANES), lambda b, t: (b, 0, 0)), seq(win)),
            scratch_shapes=[pltpu.VMEM((2, n_pages * rows, LANES), F32), pltpu.SemaphoreType.DMA((2,)),
                            pltpu.VMEM((nbp, LANES), F32), pltpu.VMEM((past, LANES), F32)]),
        out_shape=(jax.ShapeDtypeStruct((s, N_KV * hd, LANES), F32),
                   jax.ShapeDtypeStruct(win.shape, F32)),
        compiler_params=_cparams("arbitrary"), name="nsa_decode",
    )(table, qbd, gate, ckv, pages, news, win, neww, newwt)


MLSTM_CHUNK = 128
DECODE_PAD = 16
ROW_TILE = 256


def _pad_rows(a, n, value=0.0):
    return jnp.pad(a, ((0, 0), (0, n - a.shape[1])) + ((0, 0),) * (a.ndim - 2), constant_values=value)


def _to_pages(rows):
    return rows.transpose(0, 2, 1)


def _native_pages(a):
    p, tokens = a.shape[:2]
    return a.transpose(0, 2, 3, 4, 1).reshape(p, -1, tokens)


def _prompt_layer(x, prm):
    g_pre, wn, wt, bn, bt, bd, w_out, g_post, dims = prm
    mw, nw, kvw = dims
    bsz, t, d = x.shape
    hd = nw // N_HEADS
    x2 = x.reshape(bsz * t, d)
    (qkv, oz, zn, kvc, kvs, kvwin, small, ks, kw, qt, vst, vwt, smallt) = _inproj(
        x2, g_pre, wn, wt, bn, bt, mw=mw, nw=nw, kvw=kvw, tm=ROW_TILE)
    dh = mw // M_HEADS
    gates_t = smallt[:8].reshape(8, bsz, t).transpose(1, 0, 2)
    c0 = jnp.zeros((bsz, M_HEADS, dh, dh), F32)
    n0 = jnp.zeros((bsz, M_HEADS, dh), F32)
    m0 = jnp.full((bsz, M_HEADS), NEG_INF, F32)
    h_m, c, n, m = _mlstm(qkv.reshape(bsz, t, 3 * mw), small.reshape(bsz, t, LANES), gates_t,
                          c0, n0, m0, chunk=math.gcd(t, MLSTM_CHUNK))
    pages_per_step = math.gcd(t // PAGE, 64)
    table = jnp.arange(bsz * t // PAGE, dtype=jnp.int32).reshape(-1, pages_per_step)
    ckv = _compress(_to_pages(kvc.reshape(-1, PAGE, kvw)), table, bd).reshape(bsz, t // BLOCK, kvw)
    ck = ckv[:, :, :kvw // 2].astype(BF16)
    cvt = ckv[:, :, kvw // 2:].transpose(0, 2, 1).astype(BF16)
    ynt = _nsa_prompt(qt, smallt, ck, cvt, ks.reshape(bsz, t, kvw // 2), vst,
                      kw.reshape(bsz, t, kvw // 2), vwt, bsz=bsz, t=t)
    y = _outproj(x2, h_m.reshape(bsz * t, mw), oz, ynt, zn, w_out, g_post, transposed=True, tm=ROW_TILE)
    kv5 = lambda a: a.reshape(bsz, t, 2, N_KV, hd)
    wlen = min(WINDOW, t)
    return y.reshape(bsz, t, d), (kv5(kvc), kv5(kvs), kv5(kvwin)[:, t - wlen:], c, n, m)


def _sample_layer(x, cache_cmp, cache_sel, win_kv, c0, n0, m0, page_table, prm):
    g_pre, wn, wt, bn, bt, bd, w_out, g_post, dims = prm
    mw, nw, kvw = dims
    bsz, t, d = x.shape
    hd = nw // N_HEADS
    dh = mw // M_HEADS
    n_pages = page_table.shape[1]
    page = cache_cmp.shape[1]
    assert page == PAGE
    past = n_pages * page
    x2 = x.reshape(bsz * t, d)
    m_rows = bsz * t
    m_pad = -(-m_rows // ROW_TILE) * ROW_TILE
    x2p = jnp.pad(x2, ((0, m_pad - m_rows), (0, 0)))
    outs = _inproj(x2p, g_pre, wn, wt, bn, bt, mw=mw, nw=nw, kvw=kvw, tm=ROW_TILE)
    (qkv, oz, zn, kvc, kvs, kvwin, small) = [a[:m_rows] for a in outs[:7]]
    qt = outs[9][:, :m_rows]
    tp = DECODE_PAD
    small3 = small.reshape(bsz, t, LANES)
    pad_gate = jnp.concatenate([jnp.full((2 * M_HEADS // 2,), NEG_INF, F32),
                                jnp.full((2 * M_HEADS // 2,), jnp.inf, F32),
                                jnp.zeros((LANES - 2 * M_HEADS,), F32)])
    small_p = jnp.concatenate([small3, jnp.broadcast_to(pad_gate, (bsz, tp - t, LANES))], axis=1)
    gates_t = small_p[:, :, :8].transpose(0, 2, 1)
    h_m, c, n, m = _mlstm(_pad_rows(qkv.reshape(bsz, t, 3 * mw), tp), small_p, gates_t,
                          c0.astype(F32), n0.astype(F32), m0.astype(F32), chunk=tp)
    h_m = h_m[:, :t]
    ckv_old = _compress(_native_pages(cache_cmp), page_table, bd)
    new_blocks = -(-(past + t) // BLOCK) - past // BLOCK
    assert past % BLOCK == 0 and new_blocks * BLOCK <= page
    new_page = _to_pages(_pad_rows(kvc.reshape(bsz, t, kvw), page))
    new_table = jnp.arange(bsz, dtype=jnp.int32).reshape(-1, math.gcd(bsz, 64))
    ckv_new = _compress(new_page, new_table, bd).reshape(bsz, page // BLOCK, kvw)[:, :new_blocks]
    nbp = -(-(past // BLOCK + new_blocks) // 8) * 8
    ckv = _pad_rows(jnp.concatenate([ckv_old, ckv_new], axis=1), nbp)
    qd = qt.reshape(N_KV, N_REP, hd, bsz, t).transpose(3, 0, 2, 1, 4)
    qbd = jnp.zeros((bsz, N_KV, hd, N_KV, N_REP, t), BF16)
    for g in range(N_KV):
        qbd = qbd.at[:, g, :, g].set(qd[:, g])
    qbd = qbd.reshape(bsz, N_KV * hd, N_KV * N_REP * t)
    qbd = jnp.pad(qbd, ((0, 0), (0, 0), (0, LANES - qbd.shape[2])))
    g_n = small3[:, :, 2 * M_HEADS:2 * M_HEADS + 3 * N_HEADS].reshape(bsz, t, 3, N_KV * N_REP)
    gate = g_n.transpose(0, 2, 3, 1).reshape(bsz, 3, N_KV * N_REP * t)
    gate = jnp.pad(gate, ((0, 0), (0, 8 - 3), (0, LANES - gate.shape[2])))
    new_win = kvwin.reshape(bsz, t, kvw)
    new_win_t = jnp.pad(_to_pages(new_win), ((0, 0), (0, 0), (LANES - t, 0)))
    o_t, s_win = _nsa_decode(page_table, qbd, gate, ckv, _native_pages(cache_sel),
                             _pad_rows(kvs.reshape(bsz, t, kvw), 8), _native_pages(win_kv),
                             _pad_rows(new_win, 8), new_win_t, hd=hd, t_new=t)
    wb = win_kv.shape[1]
    s_win = s_win.reshape(bsz, 2, N_KV, hd, wb).transpose(0, 4, 1, 2, 3)
    o6 = o_t[:, :, :N_KV * N_REP * t].reshape(bsz, N_KV, hd, N_KV, N_REP, t)
    yn = jnp.stack([o6[:, g, :, g] for g in range(N_KV)], axis=1)
    yn = yn.transpose(0, 4, 1, 3, 2).reshape(m_rows, nw)
    pad2 = lambda a: jnp.pad(a, ((0, m_pad - m_rows), (0, 0)))
    y = _outproj(x2p, pad2(h_m.reshape(m_rows, mw)), pad2(oz), pad2(yn), pad2(zn), w_out, g_post,
                 transposed=False, tm=ROW_TILE)[:m_rows]
    kv5 = lambda a: a.reshape(bsz, t, 2, N_KV, hd)
    return y.reshape(bsz, t, d), (kv5(kvc), kv5(kvs), s_win, c, n, m)


def kernel(x_prompt, x_sample, cache_cmp_kv, cache_sel_kv, state_win_kv, state_mlstm_c, state_mlstm_n,
           state_mlstm_m, page_table, g_pre, w_in, b_i, b_f, w_cmp_k, w_cmp_v, w_out, g_post):
    depth = w_in.shape[0]
    d = x_prompt.shape[-1]
    hd = w_cmp_k.shape[-1]
    nw = N_HEADS * hd
    mw = w_out.shape[1] - nw
    kvw = 2 * N_KV * hd
    xp, xs = x_prompt, x_sample
    new_p = [[] for _ in range(6)]
    new_s = [[] for _ in range(6)]
    for layer in range(depth):
        wn, wt, bn, bt = _split_weights(w_in[layer], b_i[layer], b_f[layer], mw, nw, kvw)
        bd = _compress_weights(w_cmp_k[layer], w_cmp_v[layer])
        prm = (g_pre[layer], wn, wt, bn, bt, bd, w_out[layer], g_post[layer], (mw, nw, kvw))
        xp, st_p = _prompt_layer(xp, prm)
        xs, st_s = _sample_layer(xs, cache_cmp_kv[layer], cache_sel_kv[layer],
                                 state_win_kv[layer], state_mlstm_c[layer], state_mlstm_n[layer],
                                 state_mlstm_m[layer], page_table, prm)
        for lst, a in zip(new_p, st_p):
            lst.append(a)
        for lst, a in zip(new_s, st_s):
            lst.append(a)
    return (xp, xs) + tuple(jnp.stack(a) for a in new_p) + tuple(jnp.stack(a) for a in new_s)
```

```python
import functools
import math

import jax
import jax.numpy as jnp
from jax import lax
from jax.experimental import pallas as pl
from jax.experimental.pallas import tpu as pltpu

M_HEADS = 4
N_HEADS = 8
N_KV = 2
N_REP = N_HEADS // N_KV
BLOCK = 64
N_SELECT = 16
WINDOW = 512
Q_BLOCK = 128
RMS_EPS = 1e-6

LANES = 128
F32 = jnp.float32
BF16 = jnp.bfloat16
NEG_INF = float("-inf")
LOG2E = math.log2(math.e)
MASK_NEG = -1e30

VMEM_LIMIT_BYTES = 56 * 1024 * 1024


def _cparams(*sem):
    return pltpu.CompilerParams(dimension_semantics=sem, vmem_limit_bytes=VMEM_LIMIT_BYTES)


def _dot(a, b):
    return jnp.dot(a, b, preferred_element_type=F32)


def _dot_nt(a, b):
    return lax.dot_general(a, b, (((1,), (1,)), ((), ())), preferred_element_type=F32)


def _dot_tn(a, b):
    return lax.dot_general(a, b, (((0,), (0,)), ((), ())), preferred_element_type=F32)


def _sigmoid(x):
    return 1.0 / (1.0 + jnp.exp(-x))


def _log_sigmoid(x):
    return jnp.minimum(x, 0.0) - jnp.log1p(jnp.exp(-jnp.abs(x)))


def _inproj_kernel(x_ref, g_ref, wn_ref, wt_ref, bn_ref, bt_ref,
                   qkv_ref, oz_ref, zn_ref, small_ref, ks_ref, kw_ref,
                   qt_ref, kvct_ref, kvst_ref, kvwt_ref, vst_ref, vwt_ref, smallt_ref, *, mw, nw, kvw, scale):
    x = x_ref[...]
    ms = jnp.mean(x * x, axis=-1, keepdims=True)
    xn = ((x * lax.rsqrt(ms + RMS_EPS)) * g_ref[...]).astype(BF16)
    half = kvw // 2
    c = 0
    qkv_ref[...] = _dot(xn, wn_ref[:, c:c + 3 * mw])
    c += 3 * mw
    oz_ref[...] = _dot(xn, wn_ref[:, c:c + 2 * mw])
    c += 2 * mw
    zn_ref[...] = _dot(xn, wn_ref[:, c:c + nw])
    c += nw
    ks_ref[...] = _dot(xn, wn_ref[:, c:c + half]).astype(BF16)
    c += half
    kw_ref[...] = _dot(xn, wn_ref[:, c:c + half]).astype(BF16)
    c += half
    small_ref[...] = _dot(xn, wn_ref[:, c:c + LANES]) + bn_ref[...]
    r = 0
    qt_ref[...] = (_dot_nt(wt_ref[r:r + nw, :], xn) * scale).astype(BF16)
    r += nw
    kvct_ref[0] = _dot_nt(wt_ref[r:r + kvw, :], xn)
    r += kvw
    kvst = _dot_nt(wt_ref[r:r + kvw, :], xn)
    kvst_ref[0] = kvst
    vst_ref[0] = kvst[half:].astype(BF16)
    r += kvw
    kvwt = _dot_nt(wt_ref[r:r + kvw, :], xn)
    kvwt_ref[0] = kvwt
    vwt_ref[0] = kvwt[half:].astype(BF16)
    r += kvw
    smallt_ref[...] = _dot_nt(wt_ref[r:r + 32, :], xn) + bt_ref[...]


def _split_weights(w_in, b_i, b_f, mw, nw, kvw):
    d = w_in.shape[0]
    o = 0
    qkv_m = w_in[:, o:o + 3 * mw]; o += 3 * mw
    oz_m = w_in[:, o:o + 2 * mw]; o += 2 * mw
    w_i = w_in[:, o:o + M_HEADS]; o += M_HEADS
    w_f = w_in[:, o:o + M_HEADS]; o += M_HEADS
    q_n = w_in[:, o:o + nw]; o += nw
    kv_c = w_in[:, o:o + kvw]; o += kvw
    kv_s = w_in[:, o:o + kvw]; o += kvw
    kv_w = w_in[:, o:o + kvw]; o += kvw
    g_n = w_in[:, o:o + 3 * N_HEADS]; o += 3 * N_HEADS
    z_n = w_in[:, o:o + nw]; o += nw
    assert o == w_in.shape[1]
    n_small = 2 * M_HEADS + 3 * N_HEADS
    small = jnp.concatenate([w_i, w_f, g_n, jnp.zeros((d, LANES - n_small), w_in.dtype)], axis=1)
    half = kvw // 2
    wn = jnp.concatenate([qkv_m, oz_m, z_n, kv_s[:, :half], kv_w[:, :half], small], axis=1).astype(BF16)
    wt = jnp.concatenate([q_n, kv_c, kv_s, kv_w, small[:, :32]], axis=1).T.astype(BF16)
    bias = jnp.concatenate([b_i, b_f, jnp.zeros((LANES - 2 * M_HEADS,), F32)])
    return wn, wt, bias[None, :], bias[:32, None]


def _inproj(x2d, g_pre, wn, wt, bn, bt, *, mw, nw, kvw, tm, seq_len):
    m, d = x2d.shape
    assert m % seq_len == 0 and seq_len % tm == 0 and tm % LANES == 0
    grid = (m // tm,)
    per_seq = seq_len // tm
    half = kvw // 2
    row = lambda width: pl.BlockSpec((tm, width), lambda i: (i, 0))
    col = lambda height: pl.BlockSpec((height, tm), lambda i: (0, i))
    seq = lambda height: pl.BlockSpec((1, height, tm), lambda i: (i // per_seq, 0, i % per_seq))
    full = lambda a: pl.BlockSpec(a.shape, lambda i: (0,) * a.ndim)
    g2 = g_pre[None, :]
    nseq = m // seq_len
    out_shape = (
        jax.ShapeDtypeStruct((m, 3 * mw), F32), jax.ShapeDtypeStruct((m, 2 * mw), F32),
        jax.ShapeDtypeStruct((m, nw), F32), jax.ShapeDtypeStruct((m, LANES), F32),
        jax.ShapeDtypeStruct((m, half), BF16), jax.ShapeDtypeStruct((m, half), BF16),
        jax.ShapeDtypeStruct((nw, m), BF16),
        jax.ShapeDtypeStruct((nseq, kvw, seq_len), F32), jax.ShapeDtypeStruct((nseq, kvw, seq_len), F32),
        jax.ShapeDtypeStruct((nseq, kvw, seq_len), F32),
        jax.ShapeDtypeStruct((nseq, half, seq_len), BF16), jax.ShapeDtypeStruct((nseq, half, seq_len), BF16),
        jax.ShapeDtypeStruct((32, m), F32),
    )
    out_specs = (row(3 * mw), row(2 * mw), row(nw), row(LANES), row(half), row(half),
                 col(nw), seq(kvw), seq(kvw), seq(kvw), seq(half), seq(half), col(32))
    hd = nw // N_HEADS
    return pl.pallas_call(
        functools.partial(_inproj_kernel, mw=mw, nw=nw, kvw=kvw, scale=LOG2E * hd ** -0.5),
        grid=grid,
        in_specs=[row(d), full(g2), full(wn), full(wt), full(bn), full(bt)],
        out_specs=out_specs, out_shape=out_shape,
        compiler_params=_cparams("arbitrary"), name="inproj",
    )(x2d, g2, wn, wt, bn, bt)


def _mlstm_kernel(qkv_ref, small_ref, smallt_ref, c0_ref, n0_ref, m0_ref,
                  h_ref, c_ref, n_ref, m_ref, c_s, n_s, m_s, *, chunk, dh):
    ci = pl.program_id(1)
    nh = M_HEADS
    mw = nh * dh

    @pl.when(ci == 0)
    def _():
        c_s[...] = c0_ref[0]
        n_s[...] = n0_ref[0]
        m_s[...] = m0_ref[0]

    t_idx = lax.broadcasted_iota(jnp.int32, (chunk, chunk), 0)
    s_idx = lax.broadcasted_iota(jnp.int32, (chunk, chunk), 1)
    causal = s_idx <= t_idx
    scale = dh ** -0.5
    for h in range(nh):
        q = qkv_ref[0, :, h * dh:(h + 1) * dh]
        k = qkv_ref[0, :, mw + h * dh:mw + (h + 1) * dh] * scale
        v = qkv_ref[0, :, 2 * mw + h * dh:2 * mw + (h + 1) * dh]
        ig_c = small_ref[0, :, h:h + 1]
        lf_c = _log_sigmoid(small_ref[0, :, nh + h:nh + h + 1])
        ig_r = smallt_ref[0, h:h + 1, :]
        lf_r = _log_sigmoid(smallt_ref[0, nh + h:nh + h + 1, :])
        b_c = jnp.sum(jnp.where(causal, lf_r, 0.0), axis=1, keepdims=True)
        b_r = jnp.sum(jnp.where(t_idx <= s_idx, lf_c, 0.0), axis=0, keepdims=True)
        m_prev = m_s[h:h + 1, 0:1]
        n_prev = n_s[h:h + 1, :]
        c_prev = c_s[h]
        inter = b_c + m_prev
        dmat = jnp.where(causal, (b_c - b_r) + ig_r, NEG_INF)
        mt = jnp.maximum(inter, jnp.max(dmat, axis=1, keepdims=True))
        w_d = jnp.exp(dmat - mt)
        w_i = jnp.exp(inter - mt)
        q_b = q.astype(BF16)
        k_b = k.astype(BF16)
        qk = _dot_nt(q_b, k_b) * w_d
        num = w_i * _dot_nt(q_b, c_prev.astype(BF16)) + _dot(qk.astype(BF16), v.astype(BF16))
        den = w_i * jnp.sum(q * n_prev, axis=1, keepdims=True) + jnp.sum(qk, axis=1, keepdims=True)
        h_ref[0, :, h * dh:(h + 1) * dh] = num / jnp.maximum(jnp.abs(den), jnp.exp(-mt))
        b_last = b_c[chunk - 1:chunk, :]
        m_last = mt[chunk - 1:chunk, :]
        wi_last = w_i[chunk - 1:chunk, :]
        w_last = jnp.exp(((b_last - b_c) + ig_c) - m_last)
        c_s[h] = wi_last * c_prev + _dot_tn((v * w_last).astype(BF16), k_b)
        n_s[h:h + 1, :] = wi_last * n_prev + jnp.sum(k * w_last, axis=0, keepdims=True)
        m_s[h:h + 1, :] = jnp.broadcast_to(m_last, (1, LANES))

    @pl.when(ci == pl.num_programs(1) - 1)
    def _():
        c_ref[0] = c_s[...]
        n_ref[0] = n_s[...]
        m_ref[0] = m_s[...]


def _mlstm(qkv, small, smallt, c0, n0, m0, *, chunk):
    s, t, mw3 = qkv.shape
    mw = mw3 // 3
    dh = mw // M_HEADS
    assert t % chunk == 0
    nc = t // chunk
    m0b = jnp.broadcast_to(m0[:, :, None], (s, M_HEADS, LANES))
    state = lambda *tail: pl.BlockSpec((1,) + tail, lambda b, c: (b,) + (0,) * len(tail))
    h, c, n, m = pl.pallas_call(
        functools.partial(_mlstm_kernel, chunk=chunk, dh=dh),
        grid=(s, nc),
        in_specs=[pl.BlockSpec((1, chunk, mw3), lambda b, c: (b, c, 0)),
                  pl.BlockSpec((1, chunk, LANES), lambda b, c: (b, c, 0)),
                  pl.BlockSpec((1, 8, chunk), lambda b, c: (b, 0, c)),
                  state(M_HEADS, dh, dh), state(M_HEADS, dh), state(M_HEADS, LANES)],
        out_specs=(pl.BlockSpec((1, chunk, mw), lambda b, c: (b, c, 0)),
                   state(M_HEADS, dh, dh), state(M_HEADS, dh), state(M_HEADS, LANES)),
        out_shape=(jax.ShapeDtypeStruct((s, t, mw), F32),
                   jax.ShapeDtypeStruct((s, M_HEADS, dh, dh), F32),
                   jax.ShapeDtypeStruct((s, M_HEADS, dh), F32),
                   jax.ShapeDtypeStruct((s, M_HEADS, LANES), F32)),
        scratch_shapes=[pltpu.VMEM((M_HEADS, dh, dh), F32), pltpu.VMEM((M_HEADS, dh), F32),
                        pltpu.VMEM((M_HEADS, LANES), F32)],
        compiler_params=_cparams("arbitrary", "arbitrary"), name="mlstm",
    )(qkv, small, smallt, c0, n0, m0b)
    return h, c, n, m[:, :, 0]


PAGE = 2 * BLOCK
PITCH_PAD = 8


def _page_copy(table_ref, pages_ref, buf_ref, sem_ref, seq, slot, p, rows, pitch, per_seq):
    page_id = table_ref[seq, p]
    if per_seq is None:
        src = pages_ref.at[page_id]
    else:
        src = pages_ref.at[lax.div(page_id, per_seq), :,
                           pl.ds(pl.multiple_of(lax.rem(page_id, per_seq) * PAGE, PAGE), PAGE)]
    return pltpu.make_async_copy(src, buf_ref.at[slot, pl.ds(pl.multiple_of(p * pitch, 8), rows), :],
                                 sem_ref.at[slot])


def _start_pages(table_ref, pages_ref, buf_ref, sem_ref, seq, slot, n_pages, rows, pitch, per_seq):
    def body(p, carry):
        _page_copy(table_ref, pages_ref, buf_ref, sem_ref, seq, slot, p, rows, pitch, per_seq).start()
        return carry
    lax.fori_loop(0, n_pages, body, 0)


def _wait_pages(table_ref, pages_ref, buf_ref, sem_ref, seq, slot, n_pages, rows, pitch, per_seq):
    def body(p, carry):
        _page_copy(table_ref, pages_ref, buf_ref, sem_ref, seq, slot, p, rows, pitch, per_seq).wait()
        return carry
    lax.fori_loop(0, n_pages, body, 0)


def _prefetch_pages(table_ref, pages_ref, buf_ref, sem_ref, n_pages, rows, pitch, per_seq):
    b = pl.program_id(0)
    slot = lax.rem(b, 2)
    fetch = functools.partial(_start_pages, table_ref, pages_ref, buf_ref, sem_ref,
                              n_pages=n_pages, rows=rows, pitch=pitch, per_seq=per_seq)

    @pl.when(b == 0)
    def _():
        fetch(seq=b, slot=slot)

    @pl.when(b + 1 < pl.num_programs(0))
    def _():
        fetch(seq=b + 1, slot=1 - slot)

    return b, slot


def _compress_kernel(table_ref, pages_ref, bd_ref, out_ref, buf_ref, sem_ref, *, n_pages, rows, pitch, hd,
                     per_seq):
    b, slot = _prefetch_pages(table_ref, pages_ref, buf_ref, sem_ref, n_pages, rows, pitch, per_seq)
    _wait_pages(table_ref, pages_ref, buf_ref, sem_ref, b, slot, n_pages, rows, pitch, per_seq)

    def body(dp, acc):
        new = []
        for kv in range(2):
            parts = []
            for g in range(N_KV):
                r0 = (kv * N_KV + g) * hd + 2 * dp
                parts.append(jnp.concatenate(
                    [buf_ref[slot, pl.ds(r0 + j, n_pages, stride=pitch), :] for j in range(2)], axis=1))
            lhs = jnp.concatenate(parts, axis=0).astype(BF16)
            new.append(acc[kv] + _dot(lhs, bd_ref[kv, dp]))
        return tuple(new)

    acc = lax.fori_loop(0, hd // 2, body, tuple(jnp.zeros((N_KV * n_pages, LANES), F32) for _ in range(2)),
                        unroll=4)
    for kv in range(2):
        out_ref[0, kv] = acc[kv]


def _compress_weights(w_ck, w_cv):
    hd = w_ck.shape[1]
    nblk = PAGE // BLOCK
    planes = []
    for w in (w_ck, w_cv):
        wd = w.reshape(BLOCK, hd // 2, 2, hd).transpose(1, 2, 0, 3)
        z = jnp.zeros((hd // 2, 2, nblk, BLOCK, nblk, hd), w.dtype)
        for n in range(nblk):
            z = z.at[:, :, n, :, n, :].set(wd)
        planes.append(z.reshape(hd // 2, 2 * PAGE, nblk * hd))
    return jnp.stack(planes).astype(BF16)


def _compress(pages, table, bd, token_major=False):
    _, rows, width = pages.shape
    assert PAGE == LANES and width % PAGE == 0 and (token_major or width == PAGE)
    s, n_pages = table.shape
    hd = rows // (2 * N_KV)
    pitch = rows + PITCH_PAD
    nblk = PAGE // BLOCK
    out = pl.pallas_call(
        functools.partial(_compress_kernel, n_pages=n_pages, rows=rows, pitch=pitch, hd=hd,
                          per_seq=width // PAGE if token_major else None),
        grid_spec=pltpu.PrefetchScalarGridSpec(
            num_scalar_prefetch=1, grid=(s,),
            in_specs=[pl.BlockSpec(memory_space=pl.ANY),
                      pl.BlockSpec(bd.shape, lambda b, t: (0, 0, 0, 0))],
            out_specs=pl.BlockSpec((1, 2, N_KV * n_pages, LANES), lambda b, t: (b, 0, 0, 0)),
            scratch_shapes=[pltpu.VMEM((2, n_pages * pitch, LANES), F32),
                            pltpu.SemaphoreType.DMA((2,))]),
        out_shape=jax.ShapeDtypeStruct((s, 2, N_KV * n_pages, LANES), F32),
        compiler_params=_cparams("arbitrary"), name="compress",
    )(table, pages, bd)
    out = out.reshape(s, 2, N_KV, n_pages, nblk, hd).transpose(0, 3, 4, 1, 2, 5)
    return out.reshape(s, n_pages * nblk, 2 * N_KV * hd)


def _masked_softmax_cols(s, mask):
    s = jnp.where(mask, s, NEG_INF)
    mx = jnp.max(s, axis=0, keepdims=True)
    mx = jnp.where(mx == NEG_INF, 0.0, mx)
    e = jnp.where(mask, jnp.exp2(s - mx), 0.0)
    tot = jnp.sum(e, axis=0, keepdims=True)
    return e / jnp.where(tot > 0, tot, 1.0)


def _select_blocks(imp, q_pos, n_valid):
    nb, nq = imp.shape
    nid = lax.broadcasted_iota(jnp.int32, (nb, nq), 0)
    cur = jnp.right_shift(q_pos, BLOCK.bit_length() - 1)
    complete = (nid + 1) * BLOCK - 1 <= q_pos
    forced = ((nid == 0) | (nid == cur) | (nid == cur - 1)) & (nid <= cur)
    score = jnp.where(forced, jnp.inf, jnp.where(complete, imp, NEG_INF))
    score = jnp.where(nid < n_valid, score, NEG_INF)

    def body(_, carry):
        score, sel = carry
        mx = jnp.max(score, axis=0, keepdims=True)
        first = jnp.min(jnp.where(score == mx, nid, nb), axis=0, keepdims=True)
        pick = nid == first
        sel = jnp.where(pick & (mx > NEG_INF), 1.0, sel)
        score = jnp.where(pick, NEG_INF, score)
        return score, sel

    _, sel = lax.fori_loop(0, min(N_SELECT, n_valid), body, (score, jnp.zeros((nb, nq), F32)))
    return sel


def _block_diag_queries(qt, hd):
    nq = qt.shape[1]
    z = jnp.zeros((hd, nq), qt.dtype)
    rows = []
    for g in range(N_KV):
        parts = []
        for gg in range(N_KV):
            for r in range(N_REP):
                parts.append(qt[(g * N_REP + r) * hd:(g * N_REP + r + 1) * hd, :] if gg == g else z)
        rows.append(jnp.concatenate(parts, axis=1))
    return jnp.concatenate(rows, axis=0)


KEY_TILE = 512
SUM_ROWS = 16


def _nsa_prompt_kernel(qt_ref, gt_ref, ck_ref, cvt_ref, ks_ref, vst_ref, kw_ref, vwt_ref,
                       out_ref, sel_ref, rhs_a, rhs_b, s_a, s_b, *, hd, nb):
    i = pl.program_id(1)
    nq = Q_BLOCK
    gl = N_REP * nq
    nl = N_KV * gl
    start = i * nq
    qbd = _block_diag_queries(qt_ref[...], hd)
    lane_q = lax.broadcasted_iota(jnp.int32, (1, nl), 1) & (nq - 1)
    q_pos = start + lane_q
    q_pos1 = start + lax.broadcasted_iota(jnp.int32, (1, nq), 1)

    s_c = _dot(ck_ref[0], qbd)
    nid = lax.broadcasted_iota(jnp.int32, (nb, nl), 0)
    p_c = _masked_softmax_cols(s_c, (nid + 1) * BLOCK - 1 <= q_pos)
    o_c, o_s, o_w = [], [], []
    for g in range(N_KV):
        pg = p_c[:, g * gl:(g + 1) * gl]
        o_c.append(_dot(cvt_ref[0, g * hd:(g + 1) * hd, :], pg.astype(BF16)))
        imp = pg[:, 0:nq]
        for r in range(1, N_REP):
            imp = imp + pg[:, r * nq:(r + 1) * nq]
        sel = _select_blocks(imp, q_pos1, nb)
        for r in range(N_REP):
            sel_ref[:, g * gl + r * nq:g * gl + (r + 1) * nq] = sel

    blocks_per_tile = KEY_TILE // BLOCK
    kw2 = N_KV * hd
    key_block = jnp.right_shift(lax.broadcasted_iota(jnp.int32, (KEY_TILE, kw2), 0), BLOCK.bit_length() - 1)
    onehot = jnp.where(key_block == lax.broadcasted_iota(jnp.int32, (KEY_TILE, kw2), 1), 1.0, 0.0).astype(BF16)
    ones_rows = jnp.ones((SUM_ROWS, KEY_TILE), BF16)
    for rhs in (rhs_a, rhs_b):
        rhs[0:kw2, :] = qbd
        rhs[kw2:2 * kw2, :] = jnp.zeros((kw2, nl), BF16)

    def scores(j, rhs, s_buf):
        off = pl.multiple_of(j * KEY_TILE, KEY_TILE)
        boff = pl.multiple_of(j * blocks_per_tile, blocks_per_tile)
        bias = jnp.where(sel_ref[pl.ds(boff, blocks_per_tile), :] > 0, 0.0, MASK_NEG)
        rhs[kw2:kw2 + 2 * blocks_per_tile, :] = jnp.concatenate(
            [bias, jnp.zeros_like(bias)], axis=0).astype(BF16)
        lhs = jnp.concatenate([ks_ref[0, pl.ds(off, KEY_TILE), :], onehot], axis=1)
        s_buf[...] = _dot(lhs, rhs[...])

    def consume(j, s_buf, carry, causal):
        m_run, acc = carry
        off = pl.multiple_of(j * KEY_TILE, KEY_TILE)
        s = s_buf[...]
        if causal:
            k_pos = off + lax.broadcasted_iota(jnp.int32, (KEY_TILE, nl), 0)
            s = jnp.where(k_pos <= q_pos, s, NEG_INF)
        m_new = jnp.maximum(m_run, jnp.max(s, axis=0, keepdims=True))
        m_safe = jnp.where(m_new < 0.5 * MASK_NEG, 0.0, m_new)
        alpha = jnp.exp2(m_run - m_safe)
        p_b = jnp.exp2(s - m_safe).astype(BF16)
        acc_new = []
        for g in range(N_KV):
            v_aug = jnp.concatenate([vst_ref[0, g * hd:(g + 1) * hd, pl.ds(off, KEY_TILE)], ones_rows], axis=0)
            pv = _dot(v_aug, p_b[:, g * gl:(g + 1) * gl])
            acc_new.append(alpha[:, g * gl:(g + 1) * gl] * acc[g] + pv)
        return m_new, tuple(acc_new)

    n_full = (start + nq - 1) // KEY_TILE
    n_pairs = n_full // 2
    init = (jnp.full((1, nl), NEG_INF, F32),
            tuple(jnp.zeros((hd + SUM_ROWS, gl), F32) for _ in range(N_KV)))
    scores(0, rhs_a, s_a)

    def pair(jj, carry):
        scores(2 * jj + 1, rhs_b, s_b)
        carry = consume(2 * jj, s_a, carry, False)
        scores(2 * jj + 2, rhs_a, s_a)
        return consume(2 * jj + 1, s_b, carry, False)

    carry = lax.fori_loop(0, n_pairs, pair, init)

    def last_two(carry):
        scores(n_full, rhs_b, s_b)
        carry = consume(n_full - 1, s_a, carry, False)
        return consume(n_full, s_b, carry, True)

    m_run, acc = lax.cond(n_full > 2 * n_pairs, last_two,
                          lambda carry: consume(n_full, s_a, carry, True), carry)
    for g in range(N_KV):
        o_s.append(acc[g][:hd] * (1.0 / acc[g][hd:hd + 1]))

    span = WINDOW + nq
    w0 = pl.multiple_of(jnp.maximum(start - WINDOW, 0), nq)
    s_w = _dot(kw_ref[0, pl.ds(w0, span), :], qbd)
    dpos = q_pos - (w0 + lax.broadcasted_iota(jnp.int32, (span, nl), 0))
    in_band = lax.bitcast_convert_type(dpos, jnp.uint32) < WINDOW
    s_w = jnp.where(in_band, s_w, NEG_INF)
    e_w = jnp.exp2(s_w - jnp.max(s_w, axis=0, keepdims=True)).astype(BF16)
    ones_w = jnp.ones((SUM_ROWS, span), BF16)
    for g in range(N_KV):
        v_aug = jnp.concatenate([vwt_ref[0, g * hd:(g + 1) * hd, pl.ds(w0, span)], ones_w], axis=0)
        ow = _dot(v_aug, e_w[:, g * gl:(g + 1) * gl])
        o_w.append(ow[:hd] * (1.0 / ow[hd:hd + 1]))

    gates = _sigmoid(gt_ref[...])
    for g in range(N_KV):
        for r in range(N_REP):
            y = jnp.zeros((hd, nq), F32)
            for br, o in enumerate((o_c, o_s, o_w)):
                row = 2 * M_HEADS + br * N_HEADS + g * N_REP + r
                y = y + gates[row:row + 1, :] * o[g][:, r * nq:(r + 1) * nq]
            out_ref[(g * N_REP + r) * hd:(g * N_REP + r + 1) * hd, :] = y


def _nsa_prompt(qt, smallt, ck, cvt, ks, vst, kw, vwt, *, bsz, t):
    nw = qt.shape[0]
    hd = nw // N_HEADS
    nb = ck.shape[1]
    nqb = t // Q_BLOCK
    tok = lambda rows: pl.BlockSpec((rows, Q_BLOCK), lambda b, i: (0, b * nqb + i))
    seq_rows = lambda a: pl.BlockSpec((1,) + a.shape[1:], lambda b, i: (b, 0, 0))
    return pl.pallas_call(
        functools.partial(_nsa_prompt_kernel, hd=hd, nb=nb),
        grid=(bsz, nqb),
        in_specs=[tok(nw), tok(32), seq_rows(ck), seq_rows(cvt),
                  seq_rows(ks), seq_rows(vst), seq_rows(kw), seq_rows(vwt)],
        out_specs=tok(nw),
        out_shape=jax.ShapeDtypeStruct((nw, bsz * t), F32),
        scratch_shapes=[pltpu.VMEM((nb, N_HEADS * Q_BLOCK), F32)]
        + [pltpu.VMEM((2 * N_KV * hd, N_HEADS * Q_BLOCK), BF16)] * 2
        + [pltpu.VMEM((KEY_TILE, N_HEADS * Q_BLOCK), F32)] * 2,
        compiler_params=_cparams("arbitrary", "arbitrary"), name="nsa_prompt",
    )(qt, smallt, ck, cvt, ks, vst, kw, vwt)


def _outproj_kernel(x_ref, h_ref, oz_ref, yn_ref, zn_ref, w_ref, g_ref, out_ref, *, mw, transposed):
    o_m = oz_ref[:, :mw]
    z_m = oz_ref[:, mw:]
    y_m = (_sigmoid(o_m) * h_ref[...]) * (z_m * _sigmoid(z_m))
    y_n = yn_ref[...].T if transposed else yn_ref[...]
    z_n = zn_ref[...]
    y_n = y_n * (z_n * _sigmoid(z_n))
    y = _dot(y_m.astype(BF16), w_ref[:mw, :]) + _dot(y_n.astype(BF16), w_ref[mw:, :])
    ms = jnp.mean(y * y, axis=-1, keepdims=True)
    out_ref[...] = x_ref[...] + (y * lax.rsqrt(ms + RMS_EPS)) * g_ref[...]


def _outproj(x2d, h_m, oz, yn, zn, w_out, g_post, *, transposed, tm):
    m, d = x2d.shape
    mw = h_m.shape[1]
    nw = zn.shape[1]
    row = lambda width: pl.BlockSpec((tm, width), lambda i: (i, 0))
    yn_spec = pl.BlockSpec((nw, tm), lambda i: (0, i)) if transposed else row(nw)
    w = w_out.astype(BF16)
    g2 = g_post[None, :]
    return pl.pallas_call(
        functools.partial(_outproj_kernel, mw=mw, transposed=transposed),
        grid=(m // tm,),
        in_specs=[row(d), row(mw), row(2 * mw), yn_spec, row(nw),
                  pl.BlockSpec(w.shape, lambda i: (0, 0)), pl.BlockSpec(g2.shape, lambda i: (0, 0))],
        out_specs=row(d), out_shape=jax.ShapeDtypeStruct((m, d), F32),
        compiler_params=_cparams("arbitrary"), name="outproj",
    )(x2d, h_m, oz, yn, zn, w, g2)


DECODE_TILE = 512
PAGES_PER_TILE = DECODE_TILE // PAGE


def _nsa_decode_kernel(table_ref, qbd_ref, gate_ref, ckv_ref, pages_ref, news_ref, win_ref, neww_ref, newwt_ref,
                       out_ref, swin_ref, buf_ref, sem_ref, bias_ref, s_ref, *, hd, n_pages, t_new, past, nbp):
    kw2 = N_KV * hd
    rows = 2 * kw2
    b, slot = _prefetch_pages(table_ref, pages_ref, buf_ref, sem_ref, n_pages, rows, rows, None)
    qbd = qbd_ref[0]
    lane = lax.broadcasted_iota(jnp.int32, (1, LANES), 1)
    q_pos = past + (lane & (t_new - 1))
    n_new_blocks = (past + t_new + BLOCK - 1) // BLOCK

    ckv = ckv_ref[0]
    s_c = _dot(ckv[:, :kw2].astype(BF16), qbd)
    nid = lax.broadcasted_iota(jnp.int32, (nbp, LANES), 0)
    p_c = _masked_softmax_cols(s_c, ((nid + 1) * BLOCK - 1 <= q_pos) & (nid < n_new_blocks))
    o_c = _dot_tn(ckv[:, kw2:].astype(BF16), p_c.astype(BF16))
    gl = N_REP * t_new
    imp = p_c
    for r in range(1, N_REP):
        imp = imp + pltpu.roll(p_c, LANES - r * t_new, 1)
    sel = _select_blocks(imp, q_pos, n_new_blocks)
    sel = jnp.where(((lane & (gl - 1)) < t_new) & (lane < N_KV * gl), sel, 0.0)
    sel_all = sel
    for r in range(1, N_REP):
        sel_all = sel_all + pltpu.roll(sel, r * t_new, 1)
    bias_ref[...] = jnp.where(sel_all > 0, 0.0, NEG_INF)

    _wait_pages(table_ref, pages_ref, buf_ref, sem_ref, b, slot, n_pages, rows, rows, None)

    blocks_per_tile = DECODE_TILE // BLOCK
    blocks_per_page = PAGE // BLOCK
    n_tiles = past // DECODE_TILE

    def scores(j, m_run):
        boff = pl.multiple_of(j * blocks_per_tile, blocks_per_tile)
        bias = bias_ref[pl.ds(boff, blocks_per_tile), :]
        for q in range(PAGES_PER_TILE):
            p = j * PAGES_PER_TILE + q
            keys_t = buf_ref[slot, pl.ds(pl.multiple_of(p * rows, rows), kw2), :].astype(BF16)
            s = _dot_tn(keys_t, qbd)
            s = jnp.concatenate(
                [s[n * BLOCK:(n + 1) * BLOCK, :] + bias[q * blocks_per_page + n:q * blocks_per_page + n + 1, :]
                 for n in range(blocks_per_page)], axis=0)
            s_ref[pl.ds(pl.multiple_of(p * PAGE, PAGE), PAGE), :] = s
            m_run = jnp.maximum(m_run, jnp.max(s, axis=0, keepdims=True))
        return m_run

    m_run = lax.fori_loop(0, n_tiles, scores, jnp.full((1, LANES), NEG_INF, F32), unroll=4)
    new_rows = news_ref[0]
    s_new = _dot(new_rows[:, :kw2].astype(BF16), qbd)
    k_pos = past + lax.broadcasted_iota(jnp.int32, (8, LANES), 0)
    ok = (k_pos <= q_pos) & (k_pos < past + t_new)
    s_new = jnp.where(ok, s_new + bias_ref[past // BLOCK:past // BLOCK + 1, :], NEG_INF)
    m_all = jnp.maximum(m_run, jnp.max(s_new, axis=0, keepdims=True))
    m_safe = jnp.where(m_all == NEG_INF, 0.0, m_all)

    def values(j, carry):
        l_run, acc = carry
        for q in range(PAGES_PER_TILE):
            p = j * PAGES_PER_TILE + q
            pr = jnp.exp2(s_ref[pl.ds(pl.multiple_of(p * PAGE, PAGE), PAGE), :] - m_safe)
            l_run = l_run + jnp.sum(pr, axis=0, keepdims=True)
            vals_t = buf_ref[slot, pl.ds(pl.multiple_of(p * rows + kw2, kw2), kw2), :].astype(BF16)
            acc = acc + _dot(vals_t, pr.astype(BF16))
        return l_run, acc

    l_run, acc = lax.fori_loop(0, n_tiles, values, (jnp.zeros((1, LANES), F32), jnp.zeros((kw2, LANES), F32)),
                               unroll=4)
    p_new = jnp.exp2(s_new - m_safe)
    l_run = l_run + jnp.sum(p_new, axis=0, keepdims=True)
    acc = acc + _dot_tn(new_rows[:, kw2:].astype(BF16), p_new.astype(BF16))
    o_s = acc / jnp.where(l_run > 0, l_run, 1.0)

    wb = win_ref.shape[2]
    neww = neww_ref[0]
    s_old = _dot_tn(win_ref[0, 0:kw2, :].astype(BF16), qbd)
    s_new = _dot(neww[:, :kw2].astype(BF16), qbd)
    kp_old = past - wb + lax.broadcasted_iota(jnp.int32, (wb, LANES), 0)
    kp_new = past + lax.broadcasted_iota(jnp.int32, (8, LANES), 0)
    ok_old = (q_pos - kp_old >= 0) & (q_pos - kp_old < WINDOW) & (kp_old >= 0)
    ok_new = (q_pos - kp_new >= 0) & (q_pos - kp_new < WINDOW) & (kp_new < past + t_new)
    s_old = jnp.where(ok_old, s_old, NEG_INF)
    s_new = jnp.where(ok_new, s_new, NEG_INF)
    mx = jnp.maximum(jnp.max(s_old, axis=0, keepdims=True), jnp.max(s_new, axis=0, keepdims=True))
    mx = jnp.where(mx == NEG_INF, 0.0, mx)
    e_old = jnp.where(ok_old, jnp.exp2(s_old - mx), 0.0)
    e_new = jnp.where(ok_new, jnp.exp2(s_new - mx), 0.0)
    tot = jnp.sum(e_old, axis=0, keepdims=True) + jnp.sum(e_new, axis=0, keepdims=True)
    inv = 1.0 / jnp.where(tot > 0, tot, 1.0)
    o_w = (_dot(win_ref[0, kw2:rows, :].astype(BF16), (e_old * inv).astype(BF16))
           + _dot_tn(neww[:, kw2:].astype(BF16), (e_new * inv).astype(BF16)))

    gates = _sigmoid(gate_ref[0])
    out_ref[0] = gates[0:1, :] * o_c + gates[1:2, :] * o_s + gates[2:3, :] * o_w

    rolled = pltpu.roll(win_ref[0], wb - t_new, 1)
    lane_w = lax.broadcasted_iota(jnp.int32, (rows, LANES), 1)
    if wb > LANES:
        swin_ref[0, :, 0:wb - LANES] = rolled[:, 0:wb - LANES]
    swin_ref[0, :, wb - LANES:wb] = jnp.where(lane_w >= LANES - t_new, newwt_ref[0], rolled[:, wb - LANES:wb])


def _nsa_decode(table, qbd, gate, ckv, pages, news, win, neww, newwt, *, hd, t_new):
    s, n_pages = table.shape
    _, rows, page = pages.shape
    past = n_pages * page
    nbp = ckv.shape[1]
    wb = win.shape[2]
    assert page == PAGE and past % DECODE_TILE == 0 and wb % LANES == 0 and rows == 2 * N_KV * hd
    seq = lambda a: pl.BlockSpec((1,) + a.shape[1:], lambda b, t: (b, 0, 0))
    return pl.pallas_call(
        functools.partial(_nsa_decode_kernel, hd=hd, n_pages=n_pages, t_new=t_new, past=past, nbp=nbp),
        grid_spec=pltpu.PrefetchScalarGridSpec(
            num_scalar_prefetch=1, grid=(s,),
            in_specs=[seq(qbd), seq(gate), seq(ckv), pl.BlockSpec(memory_space=pl.ANY),
                      seq(news), seq(win), seq(neww), seq(newwt)],
            out_specs=(pl.BlockSpec((1, N_KV * hd, LANES), lambda b, t: (b, 0, 0)), seq(win)),
            scratch_shapes=[pltpu.VMEM((2, n_pages * rows, LANES), F32), pltpu.SemaphoreType.DMA((2,)),
                            pltpu.VMEM((nbp, LANES), F32), pltpu.VMEM((past, LANES), F32)]),
        out_shape=(jax.ShapeDtypeStruct((s, N_KV * hd, LANES), F32),
                   jax.ShapeDtypeStruct(win.shape, F32)),
        compiler_params=_cparams("arbitrary"), name="nsa_decode",
    )(table, qbd, gate, ckv, pages, news, win, neww, newwt)


MLSTM_CHUNK = 128
DECODE_PAD = 16
ROW_TILE = 256


def _pad_rows(a, n, value=0.0):
    return jnp.pad(a, ((0, 0), (0, n - a.shape[1])) + ((0, 0),) * (a.ndim - 2), constant_values=value)


def _to_pages(rows):
    return rows.transpose(0, 2, 1)


def _native_pages(a):
    p, tokens = a.shape[:2]
    return a.transpose(0, 2, 3, 4, 1).reshape(p, -1, tokens)


def _from_pages(a, hd):
    p, _, tokens = a.shape
    return a.reshape(p, 2, N_KV, hd, tokens).transpose(0, 4, 1, 2, 3)


def _prompt_layer(x, prm):
    g_pre, wn, wt, bn, bt, bd, w_out, g_post, dims = prm
    mw, nw, kvw = dims
    bsz, t, d = x.shape
    hd = nw // N_HEADS
    x2 = x.reshape(bsz * t, d)
    (qkv, oz, zn, small, ks, kw, qt, kvct, kvst, kvwt, vst, vwt, smallt) = _inproj(
        x2, g_pre, wn, wt, bn, bt, mw=mw, nw=nw, kvw=kvw, tm=ROW_TILE, seq_len=t)
    dh = mw // M_HEADS
    gates_t = smallt[:8].reshape(8, bsz, t).transpose(1, 0, 2)
    c0 = jnp.zeros((bsz, M_HEADS, dh, dh), F32)
    n0 = jnp.zeros((bsz, M_HEADS, dh), F32)
    m0 = jnp.full((bsz, M_HEADS), NEG_INF, F32)
    h_m, c, n, m = _mlstm(qkv.reshape(bsz, t, 3 * mw), small.reshape(bsz, t, LANES), gates_t,
                          c0, n0, m0, chunk=math.gcd(t, MLSTM_CHUNK))
    pages_per_step = math.gcd(t // PAGE, 64)
    table = jnp.arange(bsz * t // PAGE, dtype=jnp.int32).reshape(-1, pages_per_step)
    ckv = _compress(kvct, table, bd, token_major=True).reshape(bsz, t // BLOCK, kvw)
    ck = ckv[:, :, :kvw // 2].astype(BF16)
    cvt = ckv[:, :, kvw // 2:].transpose(0, 2, 1).astype(BF16)
    ynt = _nsa_prompt(qt, smallt, ck, cvt, ks.reshape(bsz, t, kvw // 2), vst,
                      kw.reshape(bsz, t, kvw // 2), vwt, bsz=bsz, t=t)
    y = _outproj(x2, h_m.reshape(bsz * t, mw), oz, ynt, zn, w_out, g_post, transposed=True, tm=ROW_TILE)
    wlen = min(WINDOW, t)
    return y.reshape(bsz, t, d), (_from_pages(kvct, hd), _from_pages(kvst, hd),
                                  _from_pages(kvwt[:, :, t - wlen:], hd), c, n, m)


def _sample_layer(x, cache_cmp, cache_sel, win_kv, c0, n0, m0, page_table, prm):
    g_pre, wn, wt, bn, bt, bd, w_out, g_post, dims = prm
    mw, nw, kvw = dims
    bsz, t, d = x.shape
    hd = nw // N_HEADS
    dh = mw // M_HEADS
    n_pages = page_table.shape[1]
    page = cache_cmp.shape[1]
    assert page == PAGE
    past = n_pages * page
    x2 = x.reshape(bsz * t, d)
    m_rows = bsz * t
    m_pad = -(-m_rows // ROW_TILE) * ROW_TILE
    x2p = jnp.pad(x2, ((0, m_pad - m_rows), (0, 0)))
    outs = _inproj(x2p, g_pre, wn, wt, bn, bt, mw=mw, nw=nw, kvw=kvw, tm=ROW_TILE, seq_len=m_pad)
    (qkv, oz, zn, small) = [a[:m_rows] for a in outs[:4]]
    qt = outs[6][:, :m_rows]
    kvc, kvs, kvwin = [a[0, :, :m_rows].T for a in outs[7:10]]
    tp = DECODE_PAD
    small3 = small.reshape(bsz, t, LANES)
    pad_gate = jnp.concatenate([jnp.full((2 * M_HEADS // 2,), NEG_INF, F32),
                                jnp.full((2 * M_HEADS // 2,), jnp.inf, F32),
                                jnp.zeros((LANES - 2 * M_HEADS,), F32)])
    small_p = jnp.concatenate([small3, jnp.broadcast_to(pad_gate, (bsz, tp - t, LANES))], axis=1)
    gates_t = small_p[:, :, :8].transpose(0, 2, 1)
    h_m, c, n, m = _mlstm(_pad_rows(qkv.reshape(bsz, t, 3 * mw), tp), small_p, gates_t,
                          c0.astype(F32), n0.astype(F32), m0.astype(F32), chunk=tp)
    h_m = h_m[:, :t]
    ckv_old = _compress(_native_pages(cache_cmp), page_table, bd)
    new_blocks = -(-(past + t) // BLOCK) - past // BLOCK
    assert past % BLOCK == 0 and new_blocks * BLOCK <= page
    new_page = _to_pages(_pad_rows(kvc.reshape(bsz, t, kvw), page))
    new_table = jnp.arange(bsz, dtype=jnp.int32).reshape(-1, math.gcd(bsz, 64))
    ckv_new = _compress(new_page, new_table, bd).reshape(bsz, page // BLOCK, kvw)[:, :new_blocks]
    nbp = -(-(past // BLOCK + new_blocks) // 8) * 8
    ckv = _pad_rows(jnp.concatenate([ckv_old, ckv_new], axis=1), nbp)
    qd = qt.reshape(N_KV, N_REP, hd, bsz, t).transpose(3, 0, 2, 1, 4)
    qbd = jnp.zeros((bsz, N_KV, hd, N_KV, N_REP, t), BF16)
    for g in range(N_KV):
        qbd = qbd.at[:, g, :, g].set(qd[:, g])
    qbd = qbd.reshape(bsz, N_KV * hd, N_KV * N_REP * t)
    qbd = jnp.pad(qbd, ((0, 0), (0, 0), (0, LANES - qbd.shape[2])))
    g_n = small3[:, :, 2 * M_HEADS:2 * M_HEADS + 3 * N_HEADS].reshape(bsz, t, 3, N_KV * N_REP)
    gate = g_n.transpose(0, 2, 3, 1).reshape(bsz, 3, N_KV * N_REP * t)
    gate = jnp.pad(gate, ((0, 0), (0, 8 - 3), (0, LANES - gate.shape[2])))
    new_win = kvwin.reshape(bsz, t, kvw)
    new_win_t = jnp.pad(_to_pages(new_win), ((0, 0), (0, 0), (LANES - t, 0)))
    o_t, s_win = _nsa_decode(page_table, qbd, gate, ckv, _native_pages(cache_sel),
                             _pad_rows(kvs.reshape(bsz, t, kvw), 8), _native_pages(win_kv),
                             _pad_rows(new_win, 8), new_win_t, hd=hd, t_new=t)
    wb = win_kv.shape[1]
    s_win = s_win.reshape(bsz, 2, N_KV, hd, wb).transpose(0, 4, 1, 2, 3)
    o6 = o_t[:, :, :N_KV * N_REP * t].reshape(bsz, N_KV, hd, N_KV, N_REP, t)
    yn = jnp.stack([o6[:, g, :, g] for g in range(N_KV)], axis=1)
    yn = yn.transpose(0, 4, 1, 3, 2).reshape(m_rows, nw)
    pad2 = lambda a: jnp.pad(a, ((0, m_pad - m_rows), (0, 0)))
    y = _outproj(x2p, pad2(h_m.reshape(m_rows, mw)), pad2(oz), pad2(yn), pad2(zn), w_out, g_post,
                 transposed=False, tm=ROW_TILE)[:m_rows]
    kv5 = lambda a: a.reshape(bsz, t, 2, N_KV, hd)
    return y.reshape(bsz, t, d), (kv5(kvc), kv5(kvs), s_win, c, n, m)


def kernel(x_prompt, x_sample, cache_cmp_kv, cache_sel_kv, state_win_kv, state_mlstm_c, state_mlstm_n,
           state_mlstm_m, page_table, g_pre, w_in, b_i, b_f, w_cmp_k, w_cmp_v, w_out, g_post):
    depth = w_in.shape[0]
    d = x_prompt.shape[-1]
    hd = w_cmp_k.shape[-1]
    nw = N_HEADS * hd
    mw = w_out.shape[1] - nw
    kvw = 2 * N_KV * hd
    xp, xs = x_prompt, x_sample
    new_p = [[] for _ in range(6)]
    new_s = [[] for _ in range(6)]
    for layer in range(depth):
        wn, wt, bn, bt = _split_weights(w_in[layer], b_i[layer], b_f[layer], mw, nw, kvw)
        bd = _compress_weights(w_cmp_k[layer], w_cmp_v[layer])
        prm = (g_pre[layer], wn, wt, bn, bt, bd, w_out[layer], g_post[layer], (mw, nw, kvw))
        xp, st_p = _prompt_layer(xp, prm)
        xs, st_s = _sample_layer(xs, cache_cmp_kv[layer], cache_sel_kv[layer],
                                 state_win_kv[layer], state_mlstm_c[layer], state_mlstm_n[layer],
                                 state_mlstm_m[layer], page_table, prm)
        for lst, a in zip(new_p, st_p):
            lst.append(a)
        for lst, a in zip(new_s, st_s):
            lst.append(a)
    return (xp, xs) + tuple(jnp.stack(a) for a in new_p) + tuple(jnp.stack(a) for a in new_s)
```

```python
import functools
import math

import jax
import jax.numpy as jnp
from jax import lax
from jax.experimental import pallas as pl
from jax.experimental.pallas import tpu as pltpu

M_HEADS = 4
N_HEADS = 8
N_KV = 2
N_REP = N_HEADS // N_KV
BLOCK = 64
N_SELECT = 16
WINDOW = 512
Q_BLOCK = 128
RMS_EPS = 1e-6

LANES = 128
F32 = jnp.float32
BF16 = jnp.bfloat16
NEG_INF = float("-inf")
LOG2E = math.log2(math.e)
MASK_NEG = -1e30

VMEM_LIMIT_BYTES = 56 * 1024 * 1024


def _cparams(*sem):
    return pltpu.CompilerParams(dimension_semantics=sem, vmem_limit_bytes=VMEM_LIMIT_BYTES)


def _dot(a, b):
    return jnp.dot(a, b, preferred_element_type=F32)


def _dot_nt(a, b):
    return lax.dot_general(a, b, (((1,), (1,)), ((), ())), preferred_element_type=F32)


def _dot_tn(a, b):
    return lax.dot_general(a, b, (((0,), (0,)), ((), ())), preferred_element_type=F32)


def _sigmoid(x):
    return 1.0 / (1.0 + jnp.exp(-x))


def _log_sigmoid(x):
    return jnp.minimum(x, 0.0) - jnp.log1p(jnp.exp(-jnp.abs(x)))


def _inproj_kernel(x_ref, g_ref, wn_ref, wt_ref, bn_ref, bt_ref,
                   qkv_ref, oz_ref, zn_ref, small_ref, ks_ref, kw_ref,
                   qt_ref, kvct_ref, kvst_ref, kvwt_ref, vst_ref, vwt_ref, smallt_ref, *, mw, nw, kvw, scale):
    x = x_ref[...]
    ms = jnp.mean(x * x, axis=-1, keepdims=True)
    xn = ((x * lax.rsqrt(ms + RMS_EPS)) * g_ref[...]).astype(BF16)
    half = kvw // 2
    c = 0
    qkv_ref[...] = _dot(xn, wn_ref[:, c:c + 3 * mw])
    c += 3 * mw
    oz_ref[...] = _dot(xn, wn_ref[:, c:c + 2 * mw])
    c += 2 * mw
    zn_ref[...] = _dot(xn, wn_ref[:, c:c + nw])
    c += nw
    ks_ref[...] = _dot(xn, wn_ref[:, c:c + half]).astype(BF16)
    c += half
    kw_ref[...] = _dot(xn, wn_ref[:, c:c + half]).astype(BF16)
    c += half
    small_ref[...] = _dot(xn, wn_ref[:, c:c + LANES]) + bn_ref[...]
    r = 0
    qt_ref[...] = (_dot_nt(wt_ref[r:r + nw, :], xn) * scale).astype(BF16)
    r += nw
    kvct_ref[0] = _dot_nt(wt_ref[r:r + kvw, :], xn)
    r += kvw
    kvst = _dot_nt(wt_ref[r:r + kvw, :], xn)
    kvst_ref[0] = kvst
    vst_ref[0] = kvst[half:].astype(BF16)
    r += kvw
    kvwt = _dot_nt(wt_ref[r:r + kvw, :], xn)
    kvwt_ref[0] = kvwt
    vwt_ref[0] = kvwt[half:].astype(BF16)
    r += kvw
    smallt_ref[...] = _dot_nt(wt_ref[r:r + 32, :], xn) + bt_ref[...]


def _split_weights(w_in, b_i, b_f, mw, nw, kvw):
    d = w_in.shape[0]
    o = 0
    qkv_m = w_in[:, o:o + 3 * mw]; o += 3 * mw
    oz_m = w_in[:, o:o + 2 * mw]; o += 2 * mw
    w_i = w_in[:, o:o + M_HEADS]; o += M_HEADS
    w_f = w_in[:, o:o + M_HEADS]; o += M_HEADS
    q_n = w_in[:, o:o + nw]; o += nw
    kv_c = w_in[:, o:o + kvw]; o += kvw
    kv_s = w_in[:, o:o + kvw]; o += kvw
    kv_w = w_in[:, o:o + kvw]; o += kvw
    g_n = w_in[:, o:o + 3 * N_HEADS]; o += 3 * N_HEADS
    z_n = w_in[:, o:o + nw]; o += nw
    assert o == w_in.shape[1]
    n_small = 2 * M_HEADS + 3 * N_HEADS
    small = jnp.concatenate([w_i, w_f, g_n, jnp.zeros((d, LANES - n_small), w_in.dtype)], axis=1)
    half = kvw // 2
    wn = jnp.concatenate([qkv_m, oz_m, z_n, kv_s[:, :half], kv_w[:, :half], small], axis=1).astype(BF16)
    wt = jnp.concatenate([q_n, kv_c, kv_s, kv_w, small[:, :32]], axis=1).T.astype(BF16)
    bias = jnp.concatenate([b_i, b_f, jnp.zeros((LANES - 2 * M_HEADS,), F32)])
    return wn, wt, bias[None, :], bias[:32, None]


def _inproj(x2d, g_pre, wn, wt, bn, bt, *, mw, nw, kvw, tm, seq_len):
    m, d = x2d.shape
    assert m % seq_len == 0 and seq_len % tm == 0 and tm % LANES == 0
    grid = (m // tm,)
    per_seq = seq_len // tm
    half = kvw // 2
    row = lambda width: pl.BlockSpec((tm, width), lambda i: (i, 0))
    col = lambda height: pl.BlockSpec((height, tm), lambda i: (0, i))
    seq = lambda height: pl.BlockSpec((1, height, tm), lambda i: (i // per_seq, 0, i % per_seq))
    full = lambda a: pl.BlockSpec(a.shape, lambda i: (0,) * a.ndim)
    g2 = g_pre[None, :]
    nseq = m // seq_len
    out_shape = (
        jax.ShapeDtypeStruct((m, 3 * mw), F32), jax.ShapeDtypeStruct((m, 2 * mw), F32),
        jax.ShapeDtypeStruct((m, nw), F32), jax.ShapeDtypeStruct((m, LANES), F32),
        jax.ShapeDtypeStruct((m, half), BF16), jax.ShapeDtypeStruct((m, half), BF16),
        jax.ShapeDtypeStruct((nw, m), BF16),
        jax.ShapeDtypeStruct((nseq, kvw, seq_len), F32), jax.ShapeDtypeStruct((nseq, kvw, seq_len), F32),
        jax.ShapeDtypeStruct((nseq, kvw, seq_len), F32),
        jax.ShapeDtypeStruct((nseq, half, seq_len), BF16), jax.ShapeDtypeStruct((nseq, half, seq_len), BF16),
        jax.ShapeDtypeStruct((32, m), F32),
    )
    out_specs = (row(3 * mw), row(2 * mw), row(nw), row(LANES), row(half), row(half),
                 col(nw), seq(kvw), seq(kvw), seq(kvw), seq(half), seq(half), col(32))
    hd = nw // N_HEADS
    return pl.pallas_call(
        functools.partial(_inproj_kernel, mw=mw, nw=nw, kvw=kvw, scale=LOG2E * hd ** -0.5),
        grid=grid,
        in_specs=[row(d), full(g2), full(wn), full(wt), full(bn), full(bt)],
        out_specs=out_specs, out_shape=out_shape,
        compiler_params=_cparams("arbitrary"), name="inproj",
    )(x2d, g2, wn, wt, bn, bt)


def _mlstm_kernel(qkv_ref, small_ref, smallt_ref, c0_ref, n0_ref, m0_ref,
                  h_ref, c_ref, n_ref, m_ref, c_s, n_s, m_s, *, chunk, dh):
    ci = pl.program_id(1)
    nh = M_HEADS
    mw = nh * dh

    @pl.when(ci == 0)
    def _():
        c_s[...] = c0_ref[0]
        n_s[...] = n0_ref[0]
        m_s[...] = m0_ref[0]

    t_idx = lax.broadcasted_iota(jnp.int32, (chunk, chunk), 0)
    s_idx = lax.broadcasted_iota(jnp.int32, (chunk, chunk), 1)
    causal = s_idx <= t_idx
    scale = dh ** -0.5
    for h in range(nh):
        q = qkv_ref[0, :, h * dh:(h + 1) * dh]
        k = qkv_ref[0, :, mw + h * dh:mw + (h + 1) * dh] * scale
        v = qkv_ref[0, :, 2 * mw + h * dh:2 * mw + (h + 1) * dh]
        ig_c = small_ref[0, :, h:h + 1]
        lf_c = _log_sigmoid(small_ref[0, :, nh + h:nh + h + 1])
        ig_r = smallt_ref[0, h:h + 1, :]
        lf_r = _log_sigmoid(smallt_ref[0, nh + h:nh + h + 1, :])
        b_c = jnp.sum(jnp.where(causal, lf_r, 0.0), axis=1, keepdims=True)
        b_r = jnp.sum(jnp.where(t_idx <= s_idx, lf_c, 0.0), axis=0, keepdims=True)
        m_prev = m_s[h:h + 1, 0:1]
        n_prev = n_s[h:h + 1, :]
        c_prev = c_s[h]
        inter = b_c + m_prev
        dmat = jnp.where(causal, (b_c - b_r) + ig_r, NEG_INF)
        mt = jnp.maximum(inter, jnp.max(dmat, axis=1, keepdims=True))
        w_d = jnp.exp(dmat - mt)
        w_i = jnp.exp(inter - mt)
        q_b = q.astype(BF16)
        k_b = k.astype(BF16)
        qk = _dot_nt(q_b, k_b) * w_d
        num = w_i * _dot_nt(q_b, c_prev.astype(BF16)) + _dot(qk.astype(BF16), v.astype(BF16))
        den = w_i * jnp.sum(q * n_prev, axis=1, keepdims=True) + jnp.sum(qk, axis=1, keepdims=True)
        h_ref[0, :, h * dh:(h + 1) * dh] = num / jnp.maximum(jnp.abs(den), jnp.exp(-mt))
        b_last = b_c[chunk - 1:chunk, :]
        m_last = mt[chunk - 1:chunk, :]
        wi_last = w_i[chunk - 1:chunk, :]
        w_last = jnp.exp(((b_last - b_c) + ig_c) - m_last)
        c_s[h] = wi_last * c_prev + _dot_tn((v * w_last).astype(BF16), k_b)
        n_s[h:h + 1, :] = wi_last * n_prev + jnp.sum(k * w_last, axis=0, keepdims=True)
        m_s[h:h + 1, :] = jnp.broadcast_to(m_last, (1, LANES))

    @pl.when(ci == pl.num_programs(1) - 1)
    def _():
        c_ref[0] = c_s[...]
        n_ref[0] = n_s[...]
        m_ref[0] = m_s[...]


def _mlstm(qkv, small, smallt, c0, n0, m0, *, chunk):
    s, t, mw3 = qkv.shape
    mw = mw3 // 3
    dh = mw // M_HEADS
    assert t % chunk == 0
    nc = t // chunk
    m0b = jnp.broadcast_to(m0[:, :, None], (s, M_HEADS, LANES))
    state = lambda *tail: pl.BlockSpec((1,) + tail, lambda b, c: (b,) + (0,) * len(tail))
    h, c, n, m = pl.pallas_call(
        functools.partial(_mlstm_kernel, chunk=chunk, dh=dh),
        grid=(s, nc),
        in_specs=[pl.BlockSpec((1, chunk, mw3), lambda b, c: (b, c, 0)),
                  pl.BlockSpec((1, chunk, LANES), lambda b, c: (b, c, 0)),
                  pl.BlockSpec((1, 8, chunk), lambda b, c: (b, 0, c)),
                  state(M_HEADS, dh, dh), state(M_HEADS, dh), state(M_HEADS, LANES)],
        out_specs=(pl.BlockSpec((1, chunk, mw), lambda b, c: (b, c, 0)),
                   state(M_HEADS, dh, dh), state(M_HEADS, dh), state(M_HEADS, LANES)),
        out_shape=(jax.ShapeDtypeStruct((s, t, mw), F32),
                   jax.ShapeDtypeStruct((s, M_HEADS, dh, dh), F32),
                   jax.ShapeDtypeStruct((s, M_HEADS, dh), F32),
                   jax.ShapeDtypeStruct((s, M_HEADS, LANES), F32)),
        scratch_shapes=[pltpu.VMEM((M_HEADS, dh, dh), F32), pltpu.VMEM((M_HEADS, dh), F32),
                        pltpu.VMEM((M_HEADS, LANES), F32)],
        compiler_params=_cparams("arbitrary", "arbitrary"), name="mlstm",
    )(qkv, small, smallt, c0, n0, m0b)
    return h, c, n, m[:, :, 0]


PAGE = 2 * BLOCK
PITCH_PAD = 8


def _page_copy(table_ref, pages_ref, buf_ref, sem_ref, seq, slot, p, rows, pitch, per_seq):
    page_id = table_ref[seq, p]
    if per_seq is None:
        src = pages_ref.at[page_id]
    else:
        src = pages_ref.at[lax.div(page_id, per_seq), :,
                           pl.ds(pl.multiple_of(lax.rem(page_id, per_seq) * PAGE, PAGE), PAGE)]
    return pltpu.make_async_copy(src, buf_ref.at[slot, pl.ds(pl.multiple_of(p * pitch, 8), rows), :],
                                 sem_ref.at[slot])


def _start_pages(table_ref, pages_ref, buf_ref, sem_ref, seq, slot, n_pages, rows, pitch, per_seq):
    def body(p, carry):
        _page_copy(table_ref, pages_ref, buf_ref, sem_ref, seq, slot, p, rows, pitch, per_seq).start()
        return carry
    lax.fori_loop(0, n_pages, body, 0)


def _wait_pages(table_ref, pages_ref, buf_ref, sem_ref, seq, slot, n_pages, rows, pitch, per_seq):
    def body(p, carry):
        _page_copy(table_ref, pages_ref, buf_ref, sem_ref, seq, slot, p, rows, pitch, per_seq).wait()
        return carry
    lax.fori_loop(0, n_pages, body, 0)


def _prefetch_pages(table_ref, pages_ref, buf_ref, sem_ref, n_pages, rows, pitch, per_seq):
    b = pl.program_id(0)
    slot = lax.rem(b, 2)
    fetch = functools.partial(_start_pages, table_ref, pages_ref, buf_ref, sem_ref,
                              n_pages=n_pages, rows=rows, pitch=pitch, per_seq=per_seq)

    @pl.when(b == 0)
    def _():
        fetch(seq=b, slot=slot)

    @pl.when(b + 1 < pl.num_programs(0))
    def _():
        fetch(seq=b + 1, slot=1 - slot)

    return b, slot


def _compress_kernel(table_ref, pages_ref, bd_ref, out_ref, buf_ref, sem_ref, *, n_pages, rows, pitch, hd,
                     per_seq):
    b, slot = _prefetch_pages(table_ref, pages_ref, buf_ref, sem_ref, n_pages, rows, pitch, per_seq)
    _wait_pages(table_ref, pages_ref, buf_ref, sem_ref, b, slot, n_pages, rows, pitch, per_seq)

    def body(dp, acc):
        new = []
        for kv in range(2):
            parts = []
            for g in range(N_KV):
                r0 = (kv * N_KV + g) * hd + 2 * dp
                parts.append(jnp.concatenate(
                    [buf_ref[slot, pl.ds(r0 + j, n_pages, stride=pitch), :] for j in range(2)], axis=1))
            lhs = jnp.concatenate(parts, axis=0).astype(BF16)
            new.append(acc[kv] + _dot(lhs, bd_ref[kv, dp]))
        return tuple(new)

    acc = lax.fori_loop(0, hd // 2, body, tuple(jnp.zeros((N_KV * n_pages, LANES), F32) for _ in range(2)),
                        unroll=4)
    for kv in range(2):
        out_ref[0, kv] = acc[kv]


def _compress_weights(w_ck, w_cv):
    hd = w_ck.shape[1]
    nblk = PAGE // BLOCK
    planes = []
    for w in (w_ck, w_cv):
        wd = w.reshape(BLOCK, hd // 2, 2, hd).transpose(1, 2, 0, 3)
        z = jnp.zeros((hd // 2, 2, nblk, BLOCK, nblk, hd), w.dtype)
        for n in range(nblk):
            z = z.at[:, :, n, :, n, :].set(wd)
        planes.append(z.reshape(hd // 2, 2 * PAGE, nblk * hd))
    return jnp.stack(planes).astype(BF16)


def _compress(pages, table, bd, token_major=False):
    _, rows, width = pages.shape
    assert PAGE == LANES and width % PAGE == 0 and (token_major or width == PAGE)
    s, n_pages = table.shape
    hd = rows // (2 * N_KV)
    pitch = rows + PITCH_PAD
    nblk = PAGE // BLOCK
    out = pl.pallas_call(
        functools.partial(_compress_kernel, n_pages=n_pages, rows=rows, pitch=pitch, hd=hd,
                          per_seq=width // PAGE if token_major else None),
        grid_spec=pltpu.PrefetchScalarGridSpec(
            num_scalar_prefetch=1, grid=(s,),
            in_specs=[pl.BlockSpec(memory_space=pl.ANY),
                      pl.BlockSpec(bd.shape, lambda b, t: (0, 0, 0, 0))],
            out_specs=pl.BlockSpec((1, 2, N_KV * n_pages, LANES), lambda b, t: (b, 0, 0, 0)),
            scratch_shapes=[pltpu.VMEM((2, n_pages * pitch, LANES), F32),
                            pltpu.SemaphoreType.DMA((2,))]),
        out_shape=jax.ShapeDtypeStruct((s, 2, N_KV * n_pages, LANES), F32),
        compiler_params=_cparams("arbitrary"), name="compress",
    )(table, pages, bd)
    out = out.reshape(s, 2, N_KV, n_pages, nblk, hd).transpose(0, 3, 4, 1, 2, 5)
    return out.reshape(s, n_pages * nblk, 2 * N_KV * hd)


def _masked_softmax_cols(s, mask):
    s = jnp.where(mask, s, NEG_INF)
    mx = jnp.max(s, axis=0, keepdims=True)
    mx = jnp.where(mx == NEG_INF, 0.0, mx)
    e = jnp.where(mask, jnp.exp2(s - mx), 0.0)
    tot = jnp.sum(e, axis=0, keepdims=True)
    return e / jnp.where(tot > 0, tot, 1.0)


def _select_blocks(imp, q_pos, n_valid):
    nb, nq = imp.shape
    nid = lax.broadcasted_iota(jnp.int32, (nb, nq), 0)
    cur = jnp.right_shift(q_pos, BLOCK.bit_length() - 1)
    complete = (nid + 1) * BLOCK - 1 <= q_pos
    forced = ((nid == 0) | (nid == cur) | (nid == cur - 1)) & (nid <= cur)
    score = jnp.where(forced, jnp.inf, jnp.where(complete, imp, NEG_INF))
    score = jnp.where(nid < n_valid, score, NEG_INF)

    def body(_, carry):
        score, sel = carry
        mx = jnp.max(score, axis=0, keepdims=True)
        first = jnp.min(jnp.where(score == mx, nid, nb), axis=0, keepdims=True)
        pick = nid == first
        sel = jnp.where(pick & (mx > NEG_INF), 1.0, sel)
        score = jnp.where(pick, NEG_INF, score)
        return score, sel

    _, sel = lax.fori_loop(0, min(N_SELECT, n_valid), body, (score, jnp.zeros((nb, nq), F32)))
    return sel


def _block_diag_queries(qt, hd):
    nq = qt.shape[1]
    z = jnp.zeros((hd, nq), qt.dtype)
    rows = []
    for g in range(N_KV):
        parts = []
        for gg in range(N_KV):
            for r in range(N_REP):
                parts.append(qt[(g * N_REP + r) * hd:(g * N_REP + r + 1) * hd, :] if gg == g else z)
        rows.append(jnp.concatenate(parts, axis=1))
    return jnp.concatenate(rows, axis=0)


KEY_TILE = 512
SUM_ROWS = 16
CMP_CHUNK = 64


def _nsa_prompt_kernel(qt_ref, gt_ref, ck_ref, cvt_ref, ks_ref, vst_ref, kw_ref, vwt_ref,
                       out_ref, sel_ref, rhs_a, rhs_b, s_a, s_b, *, hd, nb):
    i = pl.program_id(1)
    nq = Q_BLOCK
    gl = N_REP * nq
    nl = N_KV * gl
    start = i * nq
    qbd = _block_diag_queries(qt_ref[...], hd)
    lane_q = lax.broadcasted_iota(jnp.int32, (1, nl), 1) & (nq - 1)
    q_pos = start + lane_q
    q_pos1 = start + lax.broadcasted_iota(jnp.int32, (1, nq), 1)

    def cmp_select(nbr):
        def run():
            s_c = _dot(ck_ref[0, 0:nbr, :], qbd)
            nid = lax.broadcasted_iota(jnp.int32, (nbr, nl), 0)
            p_c = _masked_softmax_cols(s_c, (nid + 1) * BLOCK - 1 <= q_pos)
            outs = []
            for g in range(N_KV):
                pg = p_c[:, g * gl:(g + 1) * gl]
                outs.append(_dot(cvt_ref[0, g * hd:(g + 1) * hd, 0:nbr], pg.astype(BF16)))
                imp = pg[:, 0:nq]
                for r in range(1, N_REP):
                    imp = imp + pg[:, r * nq:(r + 1) * nq]
                sel = _select_blocks(imp, q_pos1, nbr)
                for r in range(N_REP):
                    sel_ref[0:nbr, g * gl + r * nq:g * gl + (r + 1) * nq] = sel
            if nbr < nb:
                sel_ref[nbr:nb, :] = jnp.zeros((nb - nbr, nl), F32)
            return tuple(outs)
        return run

    chunk = min(nb, CMP_CHUNK)
    sizes = [chunk * (c + 1) for c in range(nb // chunk)]
    size_class = jnp.minimum(lax.shift_right_logical(2 * i + 1, chunk.bit_length() - 1), len(sizes) - 1)
    o_c = lax.switch(size_class, [cmp_select(n) for n in sizes])
    o_s, o_w = [], []

    blocks_per_tile = KEY_TILE // BLOCK
    kw2 = N_KV * hd
    key_block = jnp.right_shift(lax.broadcasted_iota(jnp.int32, (KEY_TILE, kw2), 0), BLOCK.bit_length() - 1)
    onehot = jnp.where(key_block == lax.broadcasted_iota(jnp.int32, (KEY_TILE, kw2), 1), 1.0, 0.0).astype(BF16)
    ones_rows = jnp.ones((SUM_ROWS, KEY_TILE), BF16)
    for rhs in (rhs_a, rhs_b):
        rhs[0:kw2, :] = qbd
        rhs[kw2:2 * kw2, :] = jnp.zeros((kw2, nl), BF16)

    def scores(j, rhs, s_buf):
        off = pl.multiple_of(j * KEY_TILE, KEY_TILE)
        boff = pl.multiple_of(j * blocks_per_tile, blocks_per_tile)
        bias = jnp.where(sel_ref[pl.ds(boff, blocks_per_tile), :] > 0, 0.0, MASK_NEG)
        rhs[kw2:kw2 + 2 * blocks_per_tile, :] = jnp.concatenate(
            [bias, jnp.zeros_like(bias)], axis=0).astype(BF16)
        lhs = jnp.concatenate([ks_ref[0, pl.ds(off, KEY_TILE), :], onehot], axis=1)
        s_buf[...] = _dot(lhs, rhs[...])

    def consume(j, s_buf, carry, causal):
        m_run, acc = carry
        off = pl.multiple_of(j * KEY_TILE, KEY_TILE)
        s = s_buf[...]
        if causal:
            k_pos = off + lax.broadcasted_iota(jnp.int32, (KEY_TILE, nl), 0)
            s = jnp.where(k_pos <= q_pos, s, NEG_INF)
        m_new = jnp.maximum(m_run, jnp.max(s, axis=0, keepdims=True))
        m_safe = jnp.where(m_new < 0.5 * MASK_NEG, 0.0, m_new)
        alpha = jnp.exp2(m_run - m_safe)
        p_b = jnp.exp2(s - m_safe).astype(BF16)
        acc_new = []
        for g in range(N_KV):
            v_aug = jnp.concatenate([vst_ref[0, g * hd:(g + 1) * hd, pl.ds(off, KEY_TILE)], ones_rows], axis=0)
            pv = _dot(v_aug, p_b[:, g * gl:(g + 1) * gl])
            acc_new.append(alpha[:, g * gl:(g + 1) * gl] * acc[g] + pv)
        return m_new, tuple(acc_new)

    n_full = (start + nq - 1) // KEY_TILE
    n_pairs = n_full // 2
    init = (jnp.full((1, nl), NEG_INF, F32),
            tuple(jnp.zeros((hd + SUM_ROWS, gl), F32) for _ in range(N_KV)))
    scores(0, rhs_a, s_a)

    def pair(jj, carry):
        scores(2 * jj + 1, rhs_b, s_b)
        carry = consume(2 * jj, s_a, carry, False)
        scores(2 * jj + 2, rhs_a, s_a)
        return consume(2 * jj + 1, s_b, carry, False)

    carry = lax.fori_loop(0, n_pairs, pair, init)

    def last_two(carry):
        scores(n_full, rhs_b, s_b)
        carry = consume(n_full - 1, s_a, carry, False)
        return consume(n_full, s_b, carry, True)

    m_run, acc = lax.cond(n_full > 2 * n_pairs, last_two,
                          lambda carry: consume(n_full, s_a, carry, True), carry)
    for g in range(N_KV):
        o_s.append(acc[g][:hd] * (1.0 / acc[g][hd:hd + 1]))

    span = WINDOW + nq
    w0 = pl.multiple_of(jnp.maximum(start - WINDOW, 0), nq)
    s_w = _dot(kw_ref[0, pl.ds(w0, span), :], qbd)
    dpos = q_pos - (w0 + lax.broadcasted_iota(jnp.int32, (span, nl), 0))
    in_band = lax.bitcast_convert_type(dpos, jnp.uint32) < WINDOW
    s_w = jnp.where(in_band, s_w, NEG_INF)
    e_w = jnp.exp2(s_w - jnp.max(s_w, axis=0, keepdims=True)).astype(BF16)
    ones_w = jnp.ones((SUM_ROWS, span), BF16)
    for g in range(N_KV):
        v_aug = jnp.concatenate([vwt_ref[0, g * hd:(g + 1) * hd, pl.ds(w0, span)], ones_w], axis=0)
        ow = _dot(v_aug, e_w[:, g * gl:(g + 1) * gl])
        o_w.append(ow[:hd] * (1.0 / ow[hd:hd + 1]))

    gates = _sigmoid(gt_ref[...])
    for g in range(N_KV):
        for r in range(N_REP):
            y = jnp.zeros((hd, nq), F32)
            for br, o in enumerate((o_c, o_s, o_w)):
                row = 2 * M_HEADS + br * N_HEADS + g * N_REP + r
                y = y + gates[row:row + 1, :] * o[g][:, r * nq:(r + 1) * nq]
            out_ref[(g * N_REP + r) * hd:(g * N_REP + r + 1) * hd, :] = y


def _nsa_prompt(qt, smallt, ck, cvt, ks, vst, kw, vwt, *, bsz, t):
    nw = qt.shape[0]
    hd = nw // N_HEADS
    nb = ck.shape[1]
    nqb = t // Q_BLOCK
    tok = lambda rows: pl.BlockSpec((rows, Q_BLOCK), lambda b, i: (0, b * nqb + i))
    seq_rows = lambda a: pl.BlockSpec((1,) + a.shape[1:], lambda b, i: (b, 0, 0))
    return pl.pallas_call(
        functools.partial(_nsa_prompt_kernel, hd=hd, nb=nb),
        grid=(bsz, nqb),
        in_specs=[tok(nw), tok(32), seq_rows(ck), seq_rows(cvt),
                  seq_rows(ks), seq_rows(vst), seq_rows(kw), seq_rows(vwt)],
        out_specs=tok(nw),
        out_shape=jax.ShapeDtypeStruct((nw, bsz * t), F32),
        scratch_shapes=[pltpu.VMEM((nb, N_HEADS * Q_BLOCK), F32)]
        + [pltpu.VMEM((2 * N_KV * hd, N_HEADS * Q_BLOCK), BF16)] * 2
        + [pltpu.VMEM((KEY_TILE, N_HEADS * Q_BLOCK), F32)] * 2,
        compiler_params=_cparams("arbitrary", "arbitrary"), name="nsa_prompt",
    )(qt, smallt, ck, cvt, ks, vst, kw, vwt)


def _outproj_kernel(x_ref, h_ref, oz_ref, yn_ref, zn_ref, w_ref, g_ref, out_ref, *, mw, transposed):
    o_m = oz_ref[:, :mw]
    z_m = oz_ref[:, mw:]
    y_m = (_sigmoid(o_m) * h_ref[...]) * (z_m * _sigmoid(z_m))
    y_n = yn_ref[...].T if transposed else yn_ref[...]
    z_n = zn_ref[...]
    y_n = y_n * (z_n * _sigmoid(z_n))
    y = _dot(y_m.astype(BF16), w_ref[:mw, :]) + _dot(y_n.astype(BF16), w_ref[mw:, :])
    ms = jnp.mean(y * y, axis=-1, keepdims=True)
    out_ref[...] = x_ref[...] + (y * lax.rsqrt(ms + RMS_EPS)) * g_ref[...]


def _outproj(x2d, h_m, oz, yn, zn, w_out, g_post, *, transposed, tm):
    m, d = x2d.shape
    mw = h_m.shape[1]
    nw = zn.shape[1]
    row = lambda width: pl.BlockSpec((tm, width), lambda i: (i, 0))
    yn_spec = pl.BlockSpec((nw, tm), lambda i: (0, i)) if transposed else row(nw)
    w = w_out.astype(BF16)
    g2 = g_post[None, :]
    return pl.pallas_call(
        functools.partial(_outproj_kernel, mw=mw, transposed=transposed),
        grid=(m // tm,),
        in_specs=[row(d), row(mw), row(2 * mw), yn_spec, row(nw),
                  pl.BlockSpec(w.shape, lambda i: (0, 0)), pl.BlockSpec(g2.shape, lambda i: (0, 0))],
        out_specs=row(d), out_shape=jax.ShapeDtypeStruct((m, d), F32),
        compiler_params=_cparams("arbitrary"), name="outproj",
    )(x2d, h_m, oz, yn, zn, w, g2)


DECODE_TILE = 512
PAGES_PER_TILE = DECODE_TILE // PAGE


def _nsa_decode_kernel(table_ref, qbd_ref, gate_ref, ckv_ref, pages_ref, news_ref, win_ref, neww_ref, newwt_ref,
                       out_ref, swin_ref, buf_ref, sem_ref, bias_ref, s_ref, *, hd, n_pages, t_new, past, nbp):
    kw2 = N_KV * hd
    rows = 2 * kw2
    b, slot = _prefetch_pages(table_ref, pages_ref, buf_ref, sem_ref, n_pages, rows, rows, None)
    qbd = qbd_ref[0]
    lane = lax.broadcasted_iota(jnp.int32, (1, LANES), 1)
    q_pos = past + (lane & (t_new - 1))
    n_new_blocks = (past + t_new + BLOCK - 1) // BLOCK

    ckv = ckv_ref[0]
    s_c = _dot(ckv[:, :kw2].astype(BF16), qbd)
    nid = lax.broadcasted_iota(jnp.int32, (nbp, LANES), 0)
    p_c = _masked_softmax_cols(s_c, ((nid + 1) * BLOCK - 1 <= q_pos) & (nid < n_new_blocks))
    o_c = _dot_tn(ckv[:, kw2:].astype(BF16), p_c.astype(BF16))
    gl = N_REP * t_new
    imp = p_c
    for r in range(1, N_REP):
        imp = imp + pltpu.roll(p_c, LANES - r * t_new, 1)
    sel = _select_blocks(imp, q_pos, n_new_blocks)
    sel = jnp.where(((lane & (gl - 1)) < t_new) & (lane < N_KV * gl), sel, 0.0)
    sel_all = sel
    for r in range(1, N_REP):
        sel_all = sel_all + pltpu.roll(sel, r * t_new, 1)
    bias_ref[...] = jnp.where(sel_all > 0, 0.0, NEG_INF)

    _wait_pages(table_ref, pages_ref, buf_ref, sem_ref, b, slot, n_pages, rows, rows, None)

    blocks_per_tile = DECODE_TILE // BLOCK
    blocks_per_page = PAGE // BLOCK
    n_tiles = past // DECODE_TILE

    def scores(j, m_run):
        boff = pl.multiple_of(j * blocks_per_tile, blocks_per_tile)
        bias = bias_ref[pl.ds(boff, blocks_per_tile), :]
        for q in range(PAGES_PER_TILE):
            p = j * PAGES_PER_TILE + q
            keys_t = buf_ref[slot, pl.ds(pl.multiple_of(p * rows, rows), kw2), :].astype(BF16)
            s = _dot_tn(keys_t, qbd)
            s = jnp.concatenate(
                [s[n * BLOCK:(n + 1) * BLOCK, :] + bias[q * blocks_per_page + n:q * blocks_per_page + n + 1, :]
                 for n in range(blocks_per_page)], axis=0)
            s_ref[pl.ds(pl.multiple_of(p * PAGE, PAGE), PAGE), :] = s
            m_run = jnp.maximum(m_run, jnp.max(s, axis=0, keepdims=True))
        return m_run

    m_run = lax.fori_loop(0, n_tiles, scores, jnp.full((1, LANES), NEG_INF, F32), unroll=4)
    new_rows = news_ref[0]
    s_new = _dot(new_rows[:, :kw2].astype(BF16), qbd)
    k_pos = past + lax.broadcasted_iota(jnp.int32, (8, LANES), 0)
    ok = (k_pos <= q_pos) & (k_pos < past + t_new)
    s_new = jnp.where(ok, s_new + bias_ref[past // BLOCK:past // BLOCK + 1, :], NEG_INF)
    m_all = jnp.maximum(m_run, jnp.max(s_new, axis=0, keepdims=True))
    m_safe = jnp.where(m_all == NEG_INF, 0.0, m_all)

    def values(j, carry):
        l_run, acc = carry
        for q in range(PAGES_PER_TILE):
            p = j * PAGES_PER_TILE + q
            pr = jnp.exp2(s_ref[pl.ds(pl.multiple_of(p * PAGE, PAGE), PAGE), :] - m_safe)
            l_run = l_run + jnp.sum(pr, axis=0, keepdims=True)
            vals_t = buf_ref[slot, pl.ds(pl.multiple_of(p * rows + kw2, kw2), kw2), :].astype(BF16)
            acc = acc + _dot(vals_t, pr.astype(BF16))
        return l_run, acc

    l_run, acc = lax.fori_loop(0, n_tiles, values, (jnp.zeros((1, LANES), F32), jnp.zeros((kw2, LANES), F32)),
                               unroll=4)
    p_new = jnp.exp2(s_new - m_safe)
    l_run = l_run + jnp.sum(p_new, axis=0, keepdims=True)
    acc = acc + _dot_tn(new_rows[:, kw2:].astype(BF16), p_new.astype(BF16))
    o_s = acc / jnp.where(l_run > 0, l_run, 1.0)

    wb = win_ref.shape[2]
    neww = neww_ref[0]
    s_old = _dot_tn(win_ref[0, 0:kw2, :].astype(BF16), qbd)
    s_new = _dot(neww[:, :kw2].astype(BF16), qbd)
    kp_old = past - wb + lax.broadcasted_iota(jnp.int32, (wb, LANES), 0)
    kp_new = past + lax.broadcasted_iota(jnp.int32, (8, LANES), 0)
    ok_old = (q_pos - kp_old >= 0) & (q_pos - kp_old < WINDOW) & (kp_old >= 0)
    ok_new = (q_pos - kp_new >= 0) & (q_pos - kp_new < WINDOW) & (kp_new < past + t_new)
    s_old = jnp.where(ok_old, s_old, NEG_INF)
    s_new = jnp.where(ok_new, s_new, NEG_INF)
    mx = jnp.maximum(jnp.max(s_old, axis=0, keepdims=True), jnp.max(s_new, axis=0, keepdims=True))
    mx = jnp.where(mx == NEG_INF, 0.0, mx)
    e_old = jnp.where(ok_old, jnp.exp2(s_old - mx), 0.0)
    e_new = jnp.where(ok_new, jnp.exp2(s_new - mx), 0.0)
    tot = jnp.sum(e_old, axis=0, keepdims=True) + jnp.sum(e_new, axis=0, keepdims=True)
    inv = 1.0 / jnp.where(tot > 0, tot, 1.0)
    o_w = (_dot(win_ref[0, kw2:rows, :].astype(BF16), (e_old * inv).astype(BF16))
           + _dot_tn(neww[:, kw2:].astype(BF16), (e_new * inv).astype(BF16)))

    gates = _sigmoid(gate_ref[0])
    out_ref[0] = gates[0:1, :] * o_c + gates[1:2, :] * o_s + gates[2:3, :] * o_w

    rolled = pltpu.roll(win_ref[0], wb - t_new, 1)
    lane_w = lax.broadcasted_iota(jnp.int32, (rows, LANES), 1)
    if wb > LANES:
        swin_ref[0, :, 0:wb - LANES] = rolled[:, 0:wb - LANES]
    swin_ref[0, :, wb - LANES:wb] = jnp.where(lane_w >= LANES - t_new, newwt_ref[0], rolled[:, wb - LANES:wb])


def _nsa_decode(table, qbd, gate, ckv, pages, news, win, neww, newwt, *, hd, t_new):
    s, n_pages = table.shape
    _, rows, page = pages.shape
    past = n_pages * page
    nbp = ckv.shape[1]
    wb = win.shape[2]
    assert page == PAGE and past % DECODE_TILE == 0 and wb % LANES == 0 and rows == 2 * N_KV * hd
    seq = lambda a: pl.BlockSpec((1,) + a.shape[1:], lambda b, t: (b, 0, 0))
    return pl.pallas_call(
        functools.partial(_nsa_decode_kernel, hd=hd, n_pages=n_pages, t_new=t_new, past=past, nbp=nbp),
        grid_spec=pltpu.PrefetchScalarGridSpec(
            num_scalar_prefetch=1, grid=(s,),
            in_specs=[seq(qbd), seq(gate), seq(ckv), pl.BlockSpec(memory_space=pl.ANY),
                      seq(news), seq(win), seq(neww), seq(newwt)],
            out_specs=(pl.BlockSpec((1, N_KV * hd, LANES), lambda b, t: (b, 0, 0)), seq(win)),
            scratch_shapes=[pltpu.VMEM((2, n_pages * rows, LANES), F32), pltpu.SemaphoreType.DMA((2,)),
                            pltpu.VMEM((nbp, LANES), F32), pltpu.VMEM((past, LANES), F32)]),
        out_shape=(jax.ShapeDtypeStruct((s, N_KV * hd, LANES), F32),
                   jax.ShapeDtypeStruct(win.shape, F32)),
        compiler_params=_cparams("arbitrary"), name="nsa_decode",
    )(table, qbd, gate, ckv, pages, news, win, neww, newwt)


MLSTM_CHUNK = 128
DECODE_PAD = 16
ROW_TILE = 256


def _pad_rows(a, n, value=0.0):
    return jnp.pad(a, ((0, 0), (0, n - a.shape[1])) + ((0, 0),) * (a.ndim - 2), constant_values=value)


def _to_pages(rows):
    return rows.transpose(0, 2, 1)


def _native_pages(a):
    p, tokens = a.shape[:2]
    return a.transpose(0, 2, 3, 4, 1).reshape(p, -1, tokens)


def _from_pages(a, hd):
    p, _, tokens = a.shape
    return a.reshape(p, 2, N_KV, hd, tokens).transpose(0, 4, 1, 2, 3)


def _prompt_layer(x, prm):
    g_pre, wn, wt, bn, bt, bd, w_out, g_post, dims = prm
    mw, nw, kvw = dims
    bsz, t, d = x.shape
    hd = nw // N_HEADS
    x2 = x.reshape(bsz * t, d)
    (qkv, oz, zn, small, ks, kw, qt, kvct, kvst, kvwt, vst, vwt, smallt) = _inproj(
        x2, g_pre, wn, wt, bn, bt, mw=mw, nw=nw, kvw=kvw, tm=ROW_TILE, seq_len=t)
    dh = mw // M_HEADS
    gates_t = smallt[:8].reshape(8, bsz, t).transpose(1, 0, 2)
    c0 = jnp.zeros((bsz, M_HEADS, dh, dh), F32)
    n0 = jnp.zeros((bsz, M_HEADS, dh), F32)
    m0 = jnp.full((bsz, M_HEADS), NEG_INF, F32)
    h_m, c, n, m = _mlstm(qkv.reshape(bsz, t, 3 * mw), small.reshape(bsz, t, LANES), gates_t,
                          c0, n0, m0, chunk=math.gcd(t, MLSTM_CHUNK))
    pages_per_step = math.gcd(t // PAGE, 64)
    table = jnp.arange(bsz * t // PAGE, dtype=jnp.int32).reshape(-1, pages_per_step)
    ckv = _compress(kvct, table, bd, token_major=True).reshape(bsz, t // BLOCK, kvw)
    ck = ckv[:, :, :kvw // 2].astype(BF16)
    cvt = ckv[:, :, kvw // 2:].transpose(0, 2, 1).astype(BF16)
    ynt = _nsa_prompt(qt, smallt, ck, cvt, ks.reshape(bsz, t, kvw // 2), vst,
                      kw.reshape(bsz, t, kvw // 2), vwt, bsz=bsz, t=t)
    y = _outproj(x2, h_m.reshape(bsz * t, mw), oz, ynt, zn, w_out, g_post, transposed=True, tm=ROW_TILE)
    wlen = min(WINDOW, t)
    return y.reshape(bsz, t, d), (_from_pages(kvct, hd), _from_pages(kvst, hd),
                                  _from_pages(kvwt[:, :, t - wlen:], hd), c, n, m)


def _sample_layer(x, cache_cmp, cache_sel, win_kv, c0, n0, m0, page_table, prm):
    g_pre, wn, wt, bn, bt, bd, w_out, g_post, dims = prm
    mw, nw, kvw = dims
    bsz, t, d = x.shape
    hd = nw // N_HEADS
    dh = mw // M_HEADS
    n_pages = page_table.shape[1]
    page = cache_cmp.shape[1]
    assert page == PAGE
    past = n_pages * page
    x2 = x.reshape(bsz * t, d)
    m_rows = bsz * t
    m_pad = -(-m_rows // ROW_TILE) * ROW_TILE
    x2p = jnp.pad(x2, ((0, m_pad - m_rows), (0, 0)))
    outs = _inproj(x2p, g_pre, wn, wt, bn, bt, mw=mw, nw=nw, kvw=kvw, tm=ROW_TILE, seq_len=m_pad)
    (qkv, oz, zn, small) = [a[:m_rows] for a in outs[:4]]
    qt = outs[6][:, :m_rows]
    kvc, kvs, kvwin = [a[0, :, :m_rows].T for a in outs[7:10]]
    tp = DECODE_PAD
    small3 = small.reshape(bsz, t, LANES)
    pad_gate = jnp.concatenate([jnp.full((2 * M_HEADS // 2,), NEG_INF, F32),
                                jnp.full((2 * M_HEADS // 2,), jnp.inf, F32),
                                jnp.zeros((LANES - 2 * M_HEADS,), F32)])
    small_p = jnp.concatenate([small3, jnp.broadcast_to(pad_gate, (bsz, tp - t, LANES))], axis=1)
    gates_t = small_p[:, :, :8].transpose(0, 2, 1)
    h_m, c, n, m = _mlstm(_pad_rows(qkv.reshape(bsz, t, 3 * mw), tp), small_p, gates_t,
                          c0.astype(F32), n0.astype(F32), m0.astype(F32), chunk=tp)
    h_m = h_m[:, :t]
    ckv_old = _compress(_native_pages(cache_cmp), page_table, bd)
    new_blocks = -(-(past + t) // BLOCK) - past // BLOCK
    assert past % BLOCK == 0 and new_blocks * BLOCK <= page
    new_page = _to_pages(_pad_rows(kvc.reshape(bsz, t, kvw), page))
    new_table = jnp.arange(bsz, dtype=jnp.int32).reshape(-1, math.gcd(bsz, 64))
    ckv_new = _compress(new_page, new_table, bd).reshape(bsz, page // BLOCK, kvw)[:, :new_blocks]
    nbp = -(-(past // BLOCK + new_blocks) // 8) * 8
    ckv = _pad_rows(jnp.concatenate([ckv_old, ckv_new], axis=1), nbp)
    qd = qt.reshape(N_KV, N_REP, hd, bsz, t).transpose(3, 0, 2, 1, 4)
    qbd = jnp.zeros((bsz, N_KV, hd, N_KV, N_REP, t), BF16)
    for g in range(N_KV):
        qbd = qbd.at[:, g, :, g].set(qd[:, g])
    qbd = qbd.reshape(bsz, N_KV * hd, N_KV * N_REP * t)
    qbd = jnp.pad(qbd, ((0, 0), (0, 0), (0, LANES - qbd.shape[2])))
    g_n = small3[:, :, 2 * M_HEADS:2 * M_HEADS + 3 * N_HEADS].reshape(bsz, t, 3, N_KV * N_REP)
    gate = g_n.transpose(0, 2, 3, 1).reshape(bsz, 3, N_KV * N_REP * t)
    gate = jnp.pad(gate, ((0, 0), (0, 8 - 3), (0, LANES - gate.shape[2])))
    new_win = kvwin.reshape(bsz, t, kvw)
    new_win_t = jnp.pad(_to_pages(new_win), ((0, 0), (0, 0), (LANES - t, 0)))
    o_t, s_win = _nsa_decode(page_table, qbd, gate, ckv, _native_pages(cache_sel),
                             _pad_rows(kvs.reshape(bsz, t, kvw), 8), _native_pages(win_kv),
                             _pad_rows(new_win, 8), new_win_t, hd=hd, t_new=t)
    wb = win_kv.shape[1]
    s_win = s_win.reshape(bsz, 2, N_KV, hd, wb).transpose(0, 4, 1, 2, 3)
    o6 = o_t[:, :, :N_KV * N_REP * t].reshape(bsz, N_KV, hd, N_KV, N_REP, t)
    yn = jnp.stack([o6[:, g, :, g] for g in range(N_KV)], axis=1)
    yn = yn.transpose(0, 4, 1, 3, 2).reshape(m_rows, nw)
    pad2 = lambda a: jnp.pad(a, ((0, m_pad - m_rows), (0, 0)))
    y = _outproj(x2p, pad2(h_m.reshape(m_rows, mw)), pad2(oz), pad2(yn), pad2(zn), w_out, g_post,
                 transposed=False, tm=ROW_TILE)[:m_rows]
    kv5 = lambda a: a.reshape(bsz, t, 2, N_KV, hd)
    return y.reshape(bsz, t, d), (kv5(kvc), kv5(kvs), s_win, c, n, m)


def kernel(x_prompt, x_sample, cache_cmp_kv, cache_sel_kv, state_win_kv, state_mlstm_c, state_mlstm_n,
           state_mlstm_m, page_table, g_pre, w_in, b_i, b_f, w_cmp_k, w_cmp_v, w_out, g_post):
    depth = w_in.shape[0]
    d = x_prompt.shape[-1]
    hd = w_cmp_k.shape[-1]
    nw = N_HEADS * hd
    mw = w_out.shape[1] - nw
    kvw = 2 * N_KV * hd
    xp, xs = x_prompt, x_sample
    new_p = [[] for _ in range(6)]
    new_s = [[] for _ in range(6)]
    for layer in range(depth):
        wn, wt, bn, bt = _split_weights(w_in[layer], b_i[layer], b_f[layer], mw, nw, kvw)
        bd = _compress_weights(w_cmp_k[layer], w_cmp_v[layer])
        prm = (g_pre[layer], wn, wt, bn, bt, bd, w_out[layer], g_post[layer], (mw, nw, kvw))
        xp, st_p = _prompt_layer(xp, prm)
        xs, st_s = _sample_layer(xs, cache_cmp_kv[layer], cache_sel_kv[layer],
                                 state_win_kv[layer], state_mlstm_c[layer], state_mlstm_n[layer],
                                 state_mlstm_m[layer], page_table, prm)
        for lst, a in zip(new_p, st_p):
            lst.append(a)
        for lst, a in zip(new_s, st_s):
            lst.append(a)
    return (xp, xs) + tuple(jnp.stack(a) for a in new_p) + tuple(jnp.stack(a) for a in new_s)
```

```python
import functools
import math

import jax
import jax.numpy as jnp
from jax import lax
from jax.experimental import pallas as pl
from jax.experimental.pallas import tpu as pltpu

M_HEADS = 4
N_HEADS = 8
N_KV = 2
N_REP = N_HEADS // N_KV
BLOCK = 64
N_SELECT = 16
WINDOW = 512
Q_BLOCK = 128
RMS_EPS = 1e-6

LANES = 128
F32 = jnp.float32
BF16 = jnp.bfloat16
NEG_INF = float("-inf")
LOG2E = math.log2(math.e)
MASK_NEG = -1e30

VMEM_LIMIT_BYTES = 56 * 1024 * 1024


def _cparams(*sem):
    return pltpu.CompilerParams(dimension_semantics=sem, vmem_limit_bytes=VMEM_LIMIT_BYTES)


def _dot(a, b):
    return jnp.dot(a, b, preferred_element_type=F32)


def _dot_nt(a, b):
    return lax.dot_general(a, b, (((1,), (1,)), ((), ())), preferred_element_type=F32)


def _dot_tn(a, b):
    return lax.dot_general(a, b, (((0,), (0,)), ((), ())), preferred_element_type=F32)


def _sigmoid(x):
    return 1.0 / (1.0 + jnp.exp(-x))


def _log_sigmoid(x):
    return jnp.minimum(x, 0.0) - jnp.log1p(jnp.exp(-jnp.abs(x)))


def _inproj_kernel(x_ref, g_ref, wn_ref, wt_ref, bn_ref, bt_ref,
                   qkv_ref, oz_ref, zn_ref, small_ref, ks_ref, kw_ref,
                   qt_ref, kvct_ref, kvst_ref, kvwt_ref, vst_ref, vwt_ref, smallt_ref, *, mw, nw, kvw, scale):
    x = x_ref[...]
    ms = jnp.mean(x * x, axis=-1, keepdims=True)
    xn = ((x * lax.rsqrt(ms + RMS_EPS)) * g_ref[...]).astype(BF16)
    half = kvw // 2
    c = 0
    qkv_ref[...] = _dot(xn, wn_ref[:, c:c + 3 * mw])
    c += 3 * mw
    oz_ref[...] = _dot(xn, wn_ref[:, c:c + 2 * mw])
    c += 2 * mw
    zn_ref[...] = _dot(xn, wn_ref[:, c:c + nw])
    c += nw
    ks_ref[...] = _dot(xn, wn_ref[:, c:c + half]).astype(BF16)
    c += half
    kw_ref[...] = _dot(xn, wn_ref[:, c:c + half]).astype(BF16)
    c += half
    small_ref[...] = _dot(xn, wn_ref[:, c:c + LANES]) + bn_ref[...]
    r = 0
    qt_ref[...] = (_dot_nt(wt_ref[r:r + nw, :], xn) * scale).astype(BF16)
    r += nw
    kvct_ref[0] = _dot_nt(wt_ref[r:r + kvw, :], xn)
    r += kvw
    kvst = _dot_nt(wt_ref[r:r + kvw, :], xn)
    kvst_ref[0] = kvst
    vst_ref[0] = kvst[half:].astype(BF16)
    r += kvw
    kvwt = _dot_nt(wt_ref[r:r + kvw, :], xn)
    kvwt_ref[0] = kvwt
    vwt_ref[0] = kvwt[half:].astype(BF16)
    r += kvw
    smallt_ref[...] = _dot_nt(wt_ref[r:r + 32, :], xn) + bt_ref[...]


def _split_weights(w_in, b_i, b_f, mw, nw, kvw):
    d = w_in.shape[0]
    o = 0
    qkv_m = w_in[:, o:o + 3 * mw]; o += 3 * mw
    oz_m = w_in[:, o:o + 2 * mw]; o += 2 * mw
    w_i = w_in[:, o:o + M_HEADS]; o += M_HEADS
    w_f = w_in[:, o:o + M_HEADS]; o += M_HEADS
    q_n = w_in[:, o:o + nw]; o += nw
    kv_c = w_in[:, o:o + kvw]; o += kvw
    kv_s = w_in[:, o:o + kvw]; o += kvw
    kv_w = w_in[:, o:o + kvw]; o += kvw
    g_n = w_in[:, o:o + 3 * N_HEADS]; o += 3 * N_HEADS
    z_n = w_in[:, o:o + nw]; o += nw
    assert o == w_in.shape[1]
    n_small = 2 * M_HEADS + 3 * N_HEADS
    small = jnp.concatenate([w_i, w_f, g_n, jnp.zeros((d, LANES - n_small), w_in.dtype)], axis=1)
    half = kvw // 2
    wn = jnp.concatenate([qkv_m, oz_m, z_n, kv_s[:, :half], kv_w[:, :half], small], axis=1).astype(BF16)
    wt = jnp.concatenate([q_n, kv_c, kv_s, kv_w, small[:, :32]], axis=1).T.astype(BF16)
    bias = jnp.concatenate([b_i, b_f, jnp.zeros((LANES - 2 * M_HEADS,), F32)])
    return wn, wt, bias[None, :], bias[:32, None]


def _inproj(x2d, g_pre, wn, wt, bn, bt, *, mw, nw, kvw, tm, seq_len):
    m, d = x2d.shape
    assert m % seq_len == 0 and seq_len % tm == 0 and tm % LANES == 0
    grid = (m // tm,)
    per_seq = seq_len // tm
    half = kvw // 2
    row = lambda width: pl.BlockSpec((tm, width), lambda i: (i, 0))
    col = lambda height: pl.BlockSpec((height, tm), lambda i: (0, i))
    seq = lambda height: pl.BlockSpec((1, height, tm), lambda i: (i // per_seq, 0, i % per_seq))
    full = lambda a: pl.BlockSpec(a.shape, lambda i: (0,) * a.ndim)
    g2 = g_pre[None, :]
    nseq = m // seq_len
    out_shape = (
        jax.ShapeDtypeStruct((m, 3 * mw), F32), jax.ShapeDtypeStruct((m, 2 * mw), F32),
        jax.ShapeDtypeStruct((m, nw), F32), jax.ShapeDtypeStruct((m, LANES), F32),
        jax.ShapeDtypeStruct((m, half), BF16), jax.ShapeDtypeStruct((m, half), BF16),
        jax.ShapeDtypeStruct((nw, m), BF16),
        jax.ShapeDtypeStruct((nseq, kvw, seq_len), F32), jax.ShapeDtypeStruct((nseq, kvw, seq_len), F32),
        jax.ShapeDtypeStruct((nseq, kvw, seq_len), F32),
        jax.ShapeDtypeStruct((nseq, half, seq_len), BF16), jax.ShapeDtypeStruct((nseq, half, seq_len), BF16),
        jax.ShapeDtypeStruct((32, m), F32),
    )
    out_specs = (row(3 * mw), row(2 * mw), row(nw), row(LANES), row(half), row(half),
                 col(nw), seq(kvw), seq(kvw), seq(kvw), seq(half), seq(half), col(32))
    hd = nw // N_HEADS
    return pl.pallas_call(
        functools.partial(_inproj_kernel, mw=mw, nw=nw, kvw=kvw, scale=LOG2E * hd ** -0.5),
        grid=grid,
        in_specs=[row(d), full(g2), full(wn), full(wt), full(bn), full(bt)],
        out_specs=out_specs, out_shape=out_shape,
        compiler_params=_cparams("arbitrary"), name="inproj",
    )(x2d, g2, wn, wt, bn, bt)


def _mlstm_kernel(qkv_ref, small_ref, smallt_ref, c0_ref, n0_ref, m0_ref,
                  h_ref, c_ref, n_ref, m_ref, c_s, n_s, m_s, *, chunk, dh):
    ci = pl.program_id(1)
    nh = M_HEADS
    mw = nh * dh

    @pl.when(ci == 0)
    def _():
        c_s[...] = c0_ref[0]
        n_s[...] = n0_ref[0]
        m_s[...] = m0_ref[0]

    t_idx = lax.broadcasted_iota(jnp.int32, (chunk, chunk), 0)
    s_idx = lax.broadcasted_iota(jnp.int32, (chunk, chunk), 1)
    causal = s_idx <= t_idx
    scale = dh ** -0.5
    for h in range(nh):
        q = qkv_ref[0, :, h * dh:(h + 1) * dh]
        k = qkv_ref[0, :, mw + h * dh:mw + (h + 1) * dh] * scale
        v = qkv_ref[0, :, 2 * mw + h * dh:2 * mw + (h + 1) * dh]
        ig_c = small_ref[0, :, h:h + 1]
        lf_c = _log_sigmoid(small_ref[0, :, nh + h:nh + h + 1])
        ig_r = smallt_ref[0, h:h + 1, :]
        lf_r = _log_sigmoid(smallt_ref[0, nh + h:nh + h + 1, :])
        b_c = jnp.sum(jnp.where(causal, lf_r, 0.0), axis=1, keepdims=True)
        b_r = jnp.sum(jnp.where(t_idx <= s_idx, lf_c, 0.0), axis=0, keepdims=True)
        m_prev = m_s[h:h + 1, 0:1]
        n_prev = n_s[h:h + 1, :]
        c_prev = c_s[h]
        inter = b_c + m_prev
        dmat = jnp.where(causal, (b_c - b_r) + ig_r, NEG_INF)
        mt = jnp.maximum(inter, jnp.max(dmat, axis=1, keepdims=True))
        w_d = jnp.exp(dmat - mt)
        w_i = jnp.exp(inter - mt)
        q_b = q.astype(BF16)
        k_b = k.astype(BF16)
        qk = _dot_nt(q_b, k_b) * w_d
        num = w_i * _dot_nt(q_b, c_prev.astype(BF16)) + _dot(qk.astype(BF16), v.astype(BF16))
        den = w_i * jnp.sum(q * n_prev, axis=1, keepdims=True) + jnp.sum(qk, axis=1, keepdims=True)
        h_ref[0, :, h * dh:(h + 1) * dh] = num / jnp.maximum(jnp.abs(den), jnp.exp(-mt))
        b_last = b_c[chunk - 1:chunk, :]
        m_last = mt[chunk - 1:chunk, :]
        wi_last = w_i[chunk - 1:chunk, :]
        w_last = jnp.exp(((b_last - b_c) + ig_c) - m_last)
        c_s[h] = wi_last * c_prev + _dot_tn((v * w_last).astype(BF16), k_b)
        n_s[h:h + 1, :] = wi_last * n_prev + jnp.sum(k * w_last, axis=0, keepdims=True)
        m_s[h:h + 1, :] = jnp.broadcast_to(m_last, (1, LANES))

    @pl.when(ci == pl.num_programs(1) - 1)
    def _():
        c_ref[0] = c_s[...]
        n_ref[0] = n_s[...]
        m_ref[0] = m_s[...]


def _mlstm(qkv, small, smallt, c0, n0, m0, *, chunk):
    s, t, mw3 = qkv.shape
    mw = mw3 // 3
    dh = mw // M_HEADS
    assert t % chunk == 0
    nc = t // chunk
    m0b = jnp.broadcast_to(m0[:, :, None], (s, M_HEADS, LANES))
    state = lambda *tail: pl.BlockSpec((1,) + tail, lambda b, c: (b,) + (0,) * len(tail))
    h, c, n, m = pl.pallas_call(
        functools.partial(_mlstm_kernel, chunk=chunk, dh=dh),
        grid=(s, nc),
        in_specs=[pl.BlockSpec((1, chunk, mw3), lambda b, c: (b, c, 0)),
                  pl.BlockSpec((1, chunk, LANES), lambda b, c: (b, c, 0)),
                  pl.BlockSpec((1, 8, chunk), lambda b, c: (b, 0, c)),
                  state(M_HEADS, dh, dh), state(M_HEADS, dh), state(M_HEADS, LANES)],
        out_specs=(pl.BlockSpec((1, chunk, mw), lambda b, c: (b, c, 0)),
                   state(M_HEADS, dh, dh), state(M_HEADS, dh), state(M_HEADS, LANES)),
        out_shape=(jax.ShapeDtypeStruct((s, t, mw), F32),
                   jax.ShapeDtypeStruct((s, M_HEADS, dh, dh), F32),
                   jax.ShapeDtypeStruct((s, M_HEADS, dh), F32),
                   jax.ShapeDtypeStruct((s, M_HEADS, LANES), F32)),
        scratch_shapes=[pltpu.VMEM((M_HEADS, dh, dh), F32), pltpu.VMEM((M_HEADS, dh), F32),
                        pltpu.VMEM((M_HEADS, LANES), F32)],
        compiler_params=_cparams("arbitrary", "arbitrary"), name="mlstm",
    )(qkv, small, smallt, c0, n0, m0b)
    return h, c, n, m[:, :, 0]


PAGE = 2 * BLOCK
PITCH_PAD = 8


def _page_copy(table_ref, pages_ref, buf_ref, sem_ref, seq, slot, p, rows, pitch, per_seq):
    page_id = table_ref[seq, p]
    if per_seq is None:
        src = pages_ref.at[page_id]
    else:
        src = pages_ref.at[lax.div(page_id, per_seq), :,
                           pl.ds(pl.multiple_of(lax.rem(page_id, per_seq) * PAGE, PAGE), PAGE)]
    return pltpu.make_async_copy(src, buf_ref.at[slot, pl.ds(pl.multiple_of(p * pitch, 8), rows), :],
                                 sem_ref.at[slot])


def _start_pages(table_ref, pages_ref, buf_ref, sem_ref, seq, slot, n_pages, rows, pitch, per_seq):
    def body(p, carry):
        _page_copy(table_ref, pages_ref, buf_ref, sem_ref, seq, slot, p, rows, pitch, per_seq).start()
        return carry
    lax.fori_loop(0, n_pages, body, 0)


def _wait_pages(table_ref, pages_ref, buf_ref, sem_ref, seq, slot, n_pages, rows, pitch, per_seq):
    def body(p, carry):
        _page_copy(table_ref, pages_ref, buf_ref, sem_ref, seq, slot, p, rows, pitch, per_seq).wait()
        return carry
    lax.fori_loop(0, n_pages, body, 0)


def _prefetch_pages(table_ref, pages_ref, buf_ref, sem_ref, n_pages, rows, pitch, per_seq):
    b = pl.program_id(0)
    slot = lax.rem(b, 2)
    fetch = functools.partial(_start_pages, table_ref, pages_ref, buf_ref, sem_ref,
                              n_pages=n_pages, rows=rows, pitch=pitch, per_seq=per_seq)

    @pl.when(b == 0)
    def _():
        fetch(seq=b, slot=slot)

    @pl.when(b + 1 < pl.num_programs(0))
    def _():
        fetch(seq=b + 1, slot=1 - slot)

    return b, slot


def _compress_kernel(table_ref, pages_ref, bd_ref, out_ref, buf_ref, sem_ref, *, n_pages, rows, pitch, hd,
                     per_seq):
    b, slot = _prefetch_pages(table_ref, pages_ref, buf_ref, sem_ref, n_pages, rows, pitch, per_seq)
    _wait_pages(table_ref, pages_ref, buf_ref, sem_ref, b, slot, n_pages, rows, pitch, per_seq)

    def body(dp, acc):
        new = []
        for kv in range(2):
            parts = []
            for g in range(N_KV):
                r0 = (kv * N_KV + g) * hd + 2 * dp
                parts.append(jnp.concatenate(
                    [buf_ref[slot, pl.ds(r0 + j, n_pages, stride=pitch), :] for j in range(2)], axis=1))
            lhs = jnp.concatenate(parts, axis=0).astype(BF16)
            new.append(acc[kv] + _dot(lhs, bd_ref[kv, dp]))
        return tuple(new)

    acc = lax.fori_loop(0, hd // 2, body, tuple(jnp.zeros((N_KV * n_pages, LANES), F32) for _ in range(2)),
                        unroll=8)
    for kv in range(2):
        out_ref[0, kv] = acc[kv]


def _compress_weights(w_ck, w_cv):
    hd = w_ck.shape[1]
    nblk = PAGE // BLOCK
    planes = []
    for w in (w_ck, w_cv):
        wd = w.reshape(BLOCK, hd // 2, 2, hd).transpose(1, 2, 0, 3)
        z = jnp.zeros((hd // 2, 2, nblk, BLOCK, nblk, hd), w.dtype)
        for n in range(nblk):
            z = z.at[:, :, n, :, n, :].set(wd)
        planes.append(z.reshape(hd // 2, 2 * PAGE, nblk * hd))
    return jnp.stack(planes).astype(BF16)


def _compress(pages, table, bd, token_major=False):
    _, rows, width = pages.shape
    assert PAGE == LANES and width % PAGE == 0 and (token_major or width == PAGE)
    s, n_pages = table.shape
    hd = rows // (2 * N_KV)
    pitch = rows + PITCH_PAD
    nblk = PAGE // BLOCK
    out = pl.pallas_call(
        functools.partial(_compress_kernel, n_pages=n_pages, rows=rows, pitch=pitch, hd=hd,
                          per_seq=width // PAGE if token_major else None),
        grid_spec=pltpu.PrefetchScalarGridSpec(
            num_scalar_prefetch=1, grid=(s,),
            in_specs=[pl.BlockSpec(memory_space=pl.ANY),
                      pl.BlockSpec(bd.shape, lambda b, t: (0, 0, 0, 0))],
            out_specs=pl.BlockSpec((1, 2, N_KV * n_pages, LANES), lambda b, t: (b, 0, 0, 0)),
            scratch_shapes=[pltpu.VMEM((2, n_pages * pitch, LANES), F32),
                            pltpu.SemaphoreType.DMA((2,))]),
        out_shape=jax.ShapeDtypeStruct((s, 2, N_KV * n_pages, LANES), F32),
        compiler_params=_cparams("arbitrary"), name="compress",
    )(table, pages, bd)
    out = out.reshape(s, 2, N_KV, n_pages, nblk, hd).transpose(0, 3, 4, 1, 2, 5)
    return out.reshape(s, n_pages * nblk, 2 * N_KV * hd)


def _masked_softmax_cols(s, mask):
    s = jnp.where(mask, s, NEG_INF)
    mx = jnp.max(s, axis=0, keepdims=True)
    mx = jnp.where(mx == NEG_INF, 0.0, mx)
    e = jnp.where(mask, jnp.exp2(s - mx), 0.0)
    tot = jnp.sum(e, axis=0, keepdims=True)
    return e / jnp.where(tot > 0, tot, 1.0)


def _select_blocks(imp, q_pos, n_valid):
    nb, nq = imp.shape
    nid = lax.broadcasted_iota(jnp.int32, (nb, nq), 0)
    cur = jnp.right_shift(q_pos, BLOCK.bit_length() - 1)
    complete = (nid + 1) * BLOCK - 1 <= q_pos
    forced = ((nid == 0) | (nid == cur) | (nid == cur - 1)) & (nid <= cur)
    score = jnp.where(forced, jnp.inf, jnp.where(complete, imp, NEG_INF))
    score = jnp.where(nid < n_valid, score, NEG_INF)

    def body(_, carry):
        score, sel = carry
        mx = jnp.max(score, axis=0, keepdims=True)
        first = jnp.min(jnp.where(score == mx, nid, nb), axis=0, keepdims=True)
        pick = nid == first
        sel = jnp.where(pick & (mx > NEG_INF), 1.0, sel)
        score = jnp.where(pick, NEG_INF, score)
        return score, sel

    _, sel = lax.fori_loop(0, min(N_SELECT, n_valid), body, (score, jnp.zeros((nb, nq), F32)))
    return sel


def _block_diag_queries(qt, hd):
    nq = qt.shape[1]
    z = jnp.zeros((hd, nq), qt.dtype)
    rows = []
    for g in range(N_KV):
        parts = []
        for gg in range(N_KV):
            for r in range(N_REP):
                parts.append(qt[(g * N_REP + r) * hd:(g * N_REP + r + 1) * hd, :] if gg == g else z)
        rows.append(jnp.concatenate(parts, axis=1))
    return jnp.concatenate(rows, axis=0)


KEY_TILE = 512
SUM_ROWS = 16
CMP_CHUNK = 32


def _nsa_prompt_kernel(qt_ref, gt_ref, ck_ref, cvt_ref, ks_ref, vst_ref, kw_ref, vwt_ref,
                       out_ref, sel_ref, rhs_a, rhs_b, s_a, s_b, *, hd, nb):
    i = pl.program_id(1)
    nq = Q_BLOCK
    gl = N_REP * nq
    nl = N_KV * gl
    start = i * nq
    qbd = _block_diag_queries(qt_ref[...], hd)
    lane_q = lax.broadcasted_iota(jnp.int32, (1, nl), 1) & (nq - 1)
    q_pos = start + lane_q
    q_pos1 = start + lax.broadcasted_iota(jnp.int32, (1, nq), 1)

    def cmp_select(nbr):
        def run():
            s_c = _dot(ck_ref[0, 0:nbr, :], qbd)
            nid = lax.broadcasted_iota(jnp.int32, (nbr, nl), 0)
            p_c = _masked_softmax_cols(s_c, (nid + 1) * BLOCK - 1 <= q_pos)
            outs = []
            for g in range(N_KV):
                pg = p_c[:, g * gl:(g + 1) * gl]
                outs.append(_dot(cvt_ref[0, g * hd:(g + 1) * hd, 0:nbr], pg.astype(BF16)))
                imp = pg[:, 0:nq]
                for r in range(1, N_REP):
                    imp = imp + pg[:, r * nq:(r + 1) * nq]
                sel = _select_blocks(imp, q_pos1, nbr)
                for r in range(N_REP):
                    sel_ref[0:nbr, g * gl + r * nq:g * gl + (r + 1) * nq] = sel
            if nbr < nb:
                sel_ref[nbr:nb, :] = jnp.zeros((nb - nbr, nl), F32)
            return tuple(outs)
        return run

    chunk = min(nb, CMP_CHUNK)
    sizes = [chunk * (c + 1) for c in range(nb // chunk)]
    size_class = jnp.minimum(lax.shift_right_logical(2 * i + 1, chunk.bit_length() - 1), len(sizes) - 1)
    o_c = lax.switch(size_class, [cmp_select(n) for n in sizes])
    o_s, o_w = [], []

    blocks_per_tile = KEY_TILE // BLOCK
    kw2 = N_KV * hd
    key_block = jnp.right_shift(lax.broadcasted_iota(jnp.int32, (KEY_TILE, kw2), 0), BLOCK.bit_length() - 1)
    onehot = jnp.where(key_block == lax.broadcasted_iota(jnp.int32, (KEY_TILE, kw2), 1), 1.0, 0.0).astype(BF16)
    ones_rows = jnp.ones((SUM_ROWS, KEY_TILE), BF16)
    for rhs in (rhs_a, rhs_b):
        rhs[0:kw2, :] = qbd
        rhs[kw2:2 * kw2, :] = jnp.zeros((kw2, nl), BF16)

    def scores(j, rhs, s_buf):
        off = pl.multiple_of(j * KEY_TILE, KEY_TILE)
        boff = pl.multiple_of(j * blocks_per_tile, blocks_per_tile)
        bias = jnp.where(sel_ref[pl.ds(boff, blocks_per_tile), :] > 0, 0.0, MASK_NEG)
        rhs[kw2:kw2 + 2 * blocks_per_tile, :] = jnp.concatenate(
            [bias, jnp.zeros_like(bias)], axis=0).astype(BF16)
        lhs = jnp.concatenate([ks_ref[0, pl.ds(off, KEY_TILE), :], onehot], axis=1)
        s_buf[...] = _dot(lhs, rhs[...])

    def consume(j, s_buf, carry, causal):
        m_run, acc = carry
        off = pl.multiple_of(j * KEY_TILE, KEY_TILE)
        s = s_buf[...]
        if causal:
            k_pos = off + lax.broadcasted_iota(jnp.int32, (KEY_TILE, nl), 0)
            s = jnp.where(k_pos <= q_pos, s, NEG_INF)
        m_new = jnp.maximum(m_run, jnp.max(s, axis=0, keepdims=True))
        m_safe = jnp.where(m_new < 0.5 * MASK_NEG, 0.0, m_new)
        alpha = jnp.exp2(m_run - m_safe)
        p_b = jnp.exp2(s - m_safe).astype(BF16)
        acc_new = []
        for g in range(N_KV):
            v_aug = jnp.concatenate([vst_ref[0, g * hd:(g + 1) * hd, pl.ds(off, KEY_TILE)], ones_rows], axis=0)
            pv = _dot(v_aug, p_b[:, g * gl:(g + 1) * gl])
            acc_new.append(alpha[:, g * gl:(g + 1) * gl] * acc[g] + pv)
        return m_new, tuple(acc_new)

    n_full = (start + nq - 1) // KEY_TILE
    n_pairs = n_full // 2
    init = (jnp.full((1, nl), NEG_INF, F32),
            tuple(jnp.zeros((hd + SUM_ROWS, gl), F32) for _ in range(N_KV)))
    scores(0, rhs_a, s_a)

    def pair(jj, carry):
        scores(2 * jj + 1, rhs_b, s_b)
        carry = consume(2 * jj, s_a, carry, False)
        scores(2 * jj + 2, rhs_a, s_a)
        return consume(2 * jj + 1, s_b, carry, False)

    carry = lax.fori_loop(0, n_pairs, pair, init)

    def last_two(carry):
        scores(n_full, rhs_b, s_b)
        carry = consume(n_full - 1, s_a, carry, False)
        return consume(n_full, s_b, carry, True)

    m_run, acc = lax.cond(n_full > 2 * n_pairs, last_two,
                          lambda carry: consume(n_full, s_a, carry, True), carry)
    for g in range(N_KV):
        o_s.append(acc[g][:hd] * (1.0 / acc[g][hd:hd + 1]))

    span = WINDOW + nq
    w0 = pl.multiple_of(jnp.maximum(start - WINDOW, 0), nq)
    s_w = _dot(kw_ref[0, pl.ds(w0, span), :], qbd)
    dpos = q_pos - (w0 + lax.broadcasted_iota(jnp.int32, (span, nl), 0))
    in_band = lax.bitcast_convert_type(dpos, jnp.uint32) < WINDOW
    s_w = jnp.where(in_band, s_w, NEG_INF)
    e_w = jnp.exp2(s_w - jnp.max(s_w, axis=0, keepdims=True)).astype(BF16)
    ones_w = jnp.ones((SUM_ROWS, span), BF16)
    for g in range(N_KV):
        v_aug = jnp.concatenate([vwt_ref[0, g * hd:(g + 1) * hd, pl.ds(w0, span)], ones_w], axis=0)
        ow = _dot(v_aug, e_w[:, g * gl:(g + 1) * gl])
        o_w.append(ow[:hd] * (1.0 / ow[hd:hd + 1]))

    gates = _sigmoid(gt_ref[...])
    for g in range(N_KV):
        for r in range(N_REP):
            y = jnp.zeros((hd, nq), F32)
            for br, o in enumerate((o_c, o_s, o_w)):
                row = 2 * M_HEADS + br * N_HEADS + g * N_REP + r
                y = y + gates[row:row + 1, :] * o[g][:, r * nq:(r + 1) * nq]
            out_ref[(g * N_REP + r) * hd:(g * N_REP + r + 1) * hd, :] = y


def _nsa_prompt(qt, smallt, ck, cvt, ks, vst, kw, vwt, *, bsz, t):
    nw = qt.shape[0]
    hd = nw // N_HEADS
    nb = ck.shape[1]
    nqb = t // Q_BLOCK
    tok = lambda rows: pl.BlockSpec((rows, Q_BLOCK), lambda b, i: (0, b * nqb + i))
    seq_rows = lambda a: pl.BlockSpec((1,) + a.shape[1:], lambda b, i: (b, 0, 0))
    return pl.pallas_call(
        functools.partial(_nsa_prompt_kernel, hd=hd, nb=nb),
        grid=(bsz, nqb),
        in_specs=[tok(nw), tok(32), seq_rows(ck), seq_rows(cvt),
                  seq_rows(ks), seq_rows(vst), seq_rows(kw), seq_rows(vwt)],
        out_specs=tok(nw),
        out_shape=jax.ShapeDtypeStruct((nw, bsz * t), F32),
        scratch_shapes=[pltpu.VMEM((nb, N_HEADS * Q_BLOCK), F32)]
        + [pltpu.VMEM((2 * N_KV * hd, N_HEADS * Q_BLOCK), BF16)] * 2
        + [pltpu.VMEM((KEY_TILE, N_HEADS * Q_BLOCK), F32)] * 2,
        compiler_params=_cparams("arbitrary", "arbitrary"), name="nsa_prompt",
    )(qt, smallt, ck, cvt, ks, vst, kw, vwt)


def _outproj_kernel(x_ref, h_ref, oz_ref, yn_ref, zn_ref, w_ref, g_ref, out_ref, *, mw, transposed):
    o_m = oz_ref[:, :mw]
    z_m = oz_ref[:, mw:]
    y_m = (_sigmoid(o_m) * h_ref[...]) * (z_m * _sigmoid(z_m))
    y_n = yn_ref[...].T if transposed else yn_ref[...]
    z_n = zn_ref[...]
    y_n = y_n * (z_n * _sigmoid(z_n))
    y = _dot(y_m.astype(BF16), w_ref[:mw, :]) + _dot(y_n.astype(BF16), w_ref[mw:, :])
    ms = jnp.mean(y * y, axis=-1, keepdims=True)
    out_ref[...] = x_ref[...] + (y * lax.rsqrt(ms + RMS_EPS)) * g_ref[...]


def _outproj(x2d, h_m, oz, yn, zn, w_out, g_post, *, transposed, tm):
    m, d = x2d.shape
    mw = h_m.shape[1]
    nw = zn.shape[1]
    row = lambda width: pl.BlockSpec((tm, width), lambda i: (i, 0))
    yn_spec = pl.BlockSpec((nw, tm), lambda i: (0, i)) if transposed else row(nw)
    w = w_out.astype(BF16)
    g2 = g_post[None, :]
    return pl.pallas_call(
        functools.partial(_outproj_kernel, mw=mw, transposed=transposed),
        grid=(m // tm,),
        in_specs=[row(d), row(mw), row(2 * mw), yn_spec, row(nw),
                  pl.BlockSpec(w.shape, lambda i: (0, 0)), pl.BlockSpec(g2.shape, lambda i: (0, 0))],
        out_specs=row(d), out_shape=jax.ShapeDtypeStruct((m, d), F32),
        compiler_params=_cparams("arbitrary"), name="outproj",
    )(x2d, h_m, oz, yn, zn, w, g2)


DECODE_TILE = 512
PAGES_PER_TILE = DECODE_TILE // PAGE


def _nsa_decode_kernel(table_ref, qbd_ref, gate_ref, ckv_ref, pages_ref, news_ref, win_ref, neww_ref, newwt_ref,
                       out_ref, swin_ref, buf_ref, sem_ref, bias_ref, s_ref, *, hd, n_pages, t_new, past, nbp):
    kw2 = N_KV * hd
    rows = 2 * kw2
    b, slot = _prefetch_pages(table_ref, pages_ref, buf_ref, sem_ref, n_pages, rows, rows, None)
    qbd = qbd_ref[0]
    lane = lax.broadcasted_iota(jnp.int32, (1, LANES), 1)
    q_pos = past + (lane & (t_new - 1))
    n_new_blocks = (past + t_new + BLOCK - 1) // BLOCK

    ckv = ckv_ref[0]
    s_c = _dot(ckv[:, :kw2].astype(BF16), qbd)
    nid = lax.broadcasted_iota(jnp.int32, (nbp, LANES), 0)
    p_c = _masked_softmax_cols(s_c, ((nid + 1) * BLOCK - 1 <= q_pos) & (nid < n_new_blocks))
    o_c = _dot_tn(ckv[:, kw2:].astype(BF16), p_c.astype(BF16))
    gl = N_REP * t_new
    imp = p_c
    for r in range(1, N_REP):
        imp = imp + pltpu.roll(p_c, LANES - r * t_new, 1)
    sel = _select_blocks(imp, q_pos, n_new_blocks)
    sel = jnp.where(((lane & (gl - 1)) < t_new) & (lane < N_KV * gl), sel, 0.0)
    sel_all = sel
    for r in range(1, N_REP):
        sel_all = sel_all + pltpu.roll(sel, r * t_new, 1)
    bias_ref[...] = jnp.where(sel_all > 0, 0.0, NEG_INF)

    _wait_pages(table_ref, pages_ref, buf_ref, sem_ref, b, slot, n_pages, rows, rows, None)

    blocks_per_tile = DECODE_TILE // BLOCK
    blocks_per_page = PAGE // BLOCK
    n_tiles = past // DECODE_TILE

    def scores(j, m_run):
        boff = pl.multiple_of(j * blocks_per_tile, blocks_per_tile)
        bias = bias_ref[pl.ds(boff, blocks_per_tile), :]
        for q in range(PAGES_PER_TILE):
            p = j * PAGES_PER_TILE + q
            keys_t = buf_ref[slot, pl.ds(pl.multiple_of(p * rows, rows), kw2), :].astype(BF16)
            s = _dot_tn(keys_t, qbd)
            s = jnp.concatenate(
                [s[n * BLOCK:(n + 1) * BLOCK, :] + bias[q * blocks_per_page + n:q * blocks_per_page + n + 1, :]
                 for n in range(blocks_per_page)], axis=0)
            s_ref[pl.ds(pl.multiple_of(p * PAGE, PAGE), PAGE), :] = s
            m_run = jnp.maximum(m_run, jnp.max(s, axis=0, keepdims=True))
        return m_run

    m_run = lax.fori_loop(0, n_tiles, scores, jnp.full((1, LANES), NEG_INF, F32), unroll=8)
    new_rows = news_ref[0]
    s_new = _dot(new_rows[:, :kw2].astype(BF16), qbd)
    k_pos = past + lax.broadcasted_iota(jnp.int32, (8, LANES), 0)
    ok = (k_pos <= q_pos) & (k_pos < past + t_new)
    s_new = jnp.where(ok, s_new + bias_ref[past // BLOCK:past // BLOCK + 1, :], NEG_INF)
    m_all = jnp.maximum(m_run, jnp.max(s_new, axis=0, keepdims=True))
    m_safe = jnp.where(m_all == NEG_INF, 0.0, m_all)

    def values(j, carry):
        l_run, acc = carry
        for q in range(PAGES_PER_TILE):
            p = j * PAGES_PER_TILE + q
            pr = jnp.exp2(s_ref[pl.ds(pl.multiple_of(p * PAGE, PAGE), PAGE), :] - m_safe)
            l_run = l_run + jnp.sum(pr, axis=0, keepdims=True)
            vals_t = buf_ref[slot, pl.ds(pl.multiple_of(p * rows + kw2, kw2), kw2), :].astype(BF16)
            acc = acc + _dot(vals_t, pr.astype(BF16))
        return l_run, acc

    l_run, acc = lax.fori_loop(0, n_tiles, values, (jnp.zeros((1, LANES), F32), jnp.zeros((kw2, LANES), F32)),
                               unroll=8)
    p_new = jnp.exp2(s_new - m_safe)
    l_run = l_run + jnp.sum(p_new, axis=0, keepdims=True)
    acc = acc + _dot_tn(new_rows[:, kw2:].astype(BF16), p_new.astype(BF16))
    o_s = acc / jnp.where(l_run > 0, l_run, 1.0)

    wb = win_ref.shape[2]
    neww = neww_ref[0]
    s_old = _dot_tn(win_ref[0, 0:kw2, :].astype(BF16), qbd)
    s_new = _dot(neww[:, :kw2].astype(BF16), qbd)
    kp_old = past - wb + lax.broadcasted_iota(jnp.int32, (wb, LANES), 0)
    kp_new = past + lax.broadcasted_iota(jnp.int32, (8, LANES), 0)
    ok_old = (q_pos - kp_old >= 0) & (q_pos - kp_old < WINDOW) & (kp_old >= 0)
    ok_new = (q_pos - kp_new >= 0) & (q_pos - kp_new < WINDOW) & (kp_new < past + t_new)
    s_old = jnp.where(ok_old, s_old, NEG_INF)
    s_new = jnp.where(ok_new, s_new, NEG_INF)
    mx = jnp.maximum(jnp.max(s_old, axis=0, keepdims=True), jnp.max(s_new, axis=0, keepdims=True))
    mx = jnp.where(mx == NEG_INF, 0.0, mx)
    e_old = jnp.where(ok_old, jnp.exp2(s_old - mx), 0.0)
    e_new = jnp.where(ok_new, jnp.exp2(s_new - mx), 0.0)
    tot = jnp.sum(e_old, axis=0, keepdims=True) + jnp.sum(e_new, axis=0, keepdims=True)
    inv = 1.0 / jnp.where(tot > 0, tot, 1.0)
    o_w = (_dot(win_ref[0, kw2:rows, :].astype(BF16), (e_old * inv).astype(BF16))
           + _dot_tn(neww[:, kw2:].astype(BF16), (e_new * inv).astype(BF16)))

    gates = _sigmoid(gate_ref[0])
    out_ref[0] = gates[0:1, :] * o_c + gates[1:2, :] * o_s + gates[2:3, :] * o_w

    rolled = pltpu.roll(win_ref[0], wb - t_new, 1)
    lane_w = lax.broadcasted_iota(jnp.int32, (rows, LANES), 1)
    if wb > LANES:
        swin_ref[0, :, 0:wb - LANES] = rolled[:, 0:wb - LANES]
    swin_ref[0, :, wb - LANES:wb] = jnp.where(lane_w >= LANES - t_new, newwt_ref[0], rolled[:, wb - LANES:wb])


def _nsa_decode(table, qbd, gate, ckv, pages, news, win, neww, newwt, *, hd, t_new):
    s, n_pages = table.shape
    _, rows, page = pages.shape
    past = n_pages * page
    nbp = ckv.shape[1]
    wb = win.shape[2]
    assert page == PAGE and past % DECODE_TILE == 0 and wb % LANES == 0 and rows == 2 * N_KV * hd
    seq = lambda a: pl.BlockSpec((1,) + a.shape[1:], lambda b, t: (b, 0, 0))
    return pl.pallas_call(
        functools.partial(_nsa_decode_kernel, hd=hd, n_pages=n_pages, t_new=t_new, past=past, nbp=nbp),
        grid_spec=pltpu.PrefetchScalarGridSpec(
            num_scalar_prefetch=1, grid=(s,),
            in_specs=[seq(qbd), seq(gate), seq(ckv), pl.BlockSpec(memory_space=pl.ANY),
                      seq(news), seq(win), seq(neww), seq(newwt)],
            out_specs=(pl.BlockSpec((1, N_KV * hd, LANES), lambda b, t: (b, 0, 0)), seq(win)),
            scratch_shapes=[pltpu.VMEM((2, n_pages * rows, LANES), F32), pltpu.SemaphoreType.DMA((2,)),
                            pltpu.VMEM((nbp, LANES), F32), pltpu.VMEM((past, LANES), F32)]),
        out_shape=(jax.ShapeDtypeStruct((s, N_KV * hd, LANES), F32),
                   jax.ShapeDtypeStruct(win.shape, F32)),
        compiler_params=_cparams("arbitrary"), name="nsa_decode",
    )(table, qbd, gate, ckv, pages, news, win, neww, newwt)


MLSTM_CHUNK = 128
DECODE_PAD = 16
ROW_TILE = 256


def _pad_rows(a, n, value=0.0):
    return jnp.pad(a, ((0, 0), (0, n - a.shape[1])) + ((0, 0),) * (a.ndim - 2), constant_values=value)


def _to_pages(rows):
    return rows.transpose(0, 2, 1)


def _native_pages(a):
    p, tokens = a.shape[:2]
    return a.transpose(0, 2, 3, 4, 1).reshape(p, -1, tokens)


def _from_pages(a, hd):
    p, _, tokens = a.shape
    return a.reshape(p, 2, N_KV, hd, tokens).transpose(0, 4, 1, 2, 3)


def _prompt_layer(x, prm):
    g_pre, wn, wt, bn, bt, bd, w_out, g_post, dims = prm
    mw, nw, kvw = dims
    bsz, t, d = x.shape
    hd = nw // N_HEADS
    x2 = x.reshape(bsz * t, d)
    (qkv, oz, zn, small, ks, kw, qt, kvct, kvst, kvwt, vst, vwt, smallt) = _inproj(
        x2, g_pre, wn, wt, bn, bt, mw=mw, nw=nw, kvw=kvw, tm=ROW_TILE, seq_len=t)
    dh = mw // M_HEADS
    gates_t = smallt[:8].reshape(8, bsz, t).transpose(1, 0, 2)
    c0 = jnp.zeros((bsz, M_HEADS, dh, dh), F32)
    n0 = jnp.zeros((bsz, M_HEADS, dh), F32)
    m0 = jnp.full((bsz, M_HEADS), NEG_INF, F32)
    h_m, c, n, m = _mlstm(qkv.reshape(bsz, t, 3 * mw), small.reshape(bsz, t, LANES), gates_t,
                          c0, n0, m0, chunk=math.gcd(t, MLSTM_CHUNK))
    pages_per_step = math.gcd(t // PAGE, 64)
    table = jnp.arange(bsz * t // PAGE, dtype=jnp.int32).reshape(-1, pages_per_step)
    ckv = _compress(kvct, table, bd, token_major=True).reshape(bsz, t // BLOCK, kvw)
    ck = ckv[:, :, :kvw // 2].astype(BF16)
    cvt = ckv[:, :, kvw // 2:].transpose(0, 2, 1).astype(BF16)
    ynt = _nsa_prompt(qt, smallt, ck, cvt, ks.reshape(bsz, t, kvw // 2), vst,
                      kw.reshape(bsz, t, kvw // 2), vwt, bsz=bsz, t=t)
    y = _outproj(x2, h_m.reshape(bsz * t, mw), oz, ynt, zn, w_out, g_post, transposed=True, tm=ROW_TILE)
    wlen = min(WINDOW, t)
    return y.reshape(bsz, t, d), (_from_pages(kvct, hd), _from_pages(kvst, hd),
                                  _from_pages(kvwt[:, :, t - wlen:], hd), c, n, m)


def _sample_layer(x, cache_cmp, cache_sel, win_kv, c0, n0, m0, page_table, prm):
    g_pre, wn, wt, bn, bt, bd, w_out, g_post, dims = prm
    mw, nw, kvw = dims
    bsz, t, d = x.shape
    hd = nw // N_HEADS
    dh = mw // M_HEADS
    n_pages = page_table.shape[1]
    page = cache_cmp.shape[1]
    assert page == PAGE
    past = n_pages * page
    x2 = x.reshape(bsz * t, d)
    m_rows = bsz * t
    m_pad = -(-m_rows // ROW_TILE) * ROW_TILE
    x2p = jnp.pad(x2, ((0, m_pad - m_rows), (0, 0)))
    outs = _inproj(x2p, g_pre, wn, wt, bn, bt, mw=mw, nw=nw, kvw=kvw, tm=ROW_TILE, seq_len=m_pad)
    (qkv, oz, zn, small) = [a[:m_rows] for a in outs[:4]]
    qt = outs[6][:, :m_rows]
    kvc, kvs, kvwin = [a[0, :, :m_rows].T for a in outs[7:10]]
    tp = DECODE_PAD
    small3 = small.reshape(bsz, t, LANES)
    pad_gate = jnp.concatenate([jnp.full((2 * M_HEADS // 2,), NEG_INF, F32),
                                jnp.full((2 * M_HEADS // 2,), jnp.inf, F32),
                                jnp.zeros((LANES - 2 * M_HEADS,), F32)])
    small_p = jnp.concatenate([small3, jnp.broadcast_to(pad_gate, (bsz, tp - t, LANES))], axis=1)
    gates_t = small_p[:, :, :8].transpose(0, 2, 1)
    h_m, c, n, m = _mlstm(_pad_rows(qkv.reshape(bsz, t, 3 * mw), tp), small_p, gates_t,
                          c0.astype(F32), n0.astype(F32), m0.astype(F32), chunk=tp)
    h_m = h_m[:, :t]
    ckv_old = _compress(_native_pages(cache_cmp), page_table, bd)
    new_blocks = -(-(past + t) // BLOCK) - past // BLOCK
    assert past % BLOCK == 0 and new_blocks * BLOCK <= page
    new_page = _to_pages(_pad_rows(kvc.reshape(bsz, t, kvw), page))
    new_table = jnp.arange(bsz, dtype=jnp.int32).reshape(-1, math.gcd(bsz, 64))
    ckv_new = _compress(new_page, new_table, bd).reshape(bsz, page // BLOCK, kvw)[:, :new_blocks]
    nbp = -(-(past // BLOCK + new_blocks) // 8) * 8
    ckv = _pad_rows(jnp.concatenate([ckv_old, ckv_new], axis=1), nbp)
    qd = qt.reshape(N_KV, N_REP, hd, bsz, t).transpose(3, 0, 2, 1, 4)
    qbd = jnp.zeros((bsz, N_KV, hd, N_KV, N_REP, t), BF16)
    for g in range(N_KV):
        qbd = qbd.at[:, g, :, g].set(qd[:, g])
    qbd = qbd.reshape(bsz, N_KV * hd, N_KV * N_REP * t)
    qbd = jnp.pad(qbd, ((0, 0), (0, 0), (0, LANES - qbd.shape[2])))
    g_n = small3[:, :, 2 * M_HEADS:2 * M_HEADS + 3 * N_HEADS].reshape(bsz, t, 3, N_KV * N_REP)
    gate = g_n.transpose(0, 2, 3, 1).reshape(bsz, 3, N_KV * N_REP * t)
    gate = jnp.pad(gate, ((0, 0), (0, 8 - 3), (0, LANES - gate.shape[2])))
    new_win = kvwin.reshape(bsz, t, kvw)
    new_win_t = jnp.pad(_to_pages(new_win), ((0, 0), (0, 0), (LANES - t, 0)))
    o_t, s_win = _nsa_decode(page_table, qbd, gate, ckv, _native_pages(cache_sel),
                             _pad_rows(kvs.reshape(bsz, t, kvw), 8), _native_pages(win_kv),
                             _pad_rows(new_win, 8), new_win_t, hd=hd, t_new=t)
    wb = win_kv.shape[1]
    s_win = s_win.reshape(bsz, 2, N_KV, hd, wb).transpose(0, 4, 1, 2, 3)
    o6 = o_t[:, :, :N_KV * N_REP * t].reshape(bsz, N_KV, hd, N_KV, N_REP, t)
    yn = jnp.stack([o6[:, g, :, g] for g in range(N_KV)], axis=1)
    yn = yn.transpose(0, 4, 1, 3, 2).reshape(m_rows, nw)
    pad2 = lambda a: jnp.pad(a, ((0, m_pad - m_rows), (0, 0)))
    y = _outproj(x2p, pad2(h_m.reshape(m_rows, mw)), pad2(oz), pad2(yn), pad2(zn), w_out, g_post,
                 transposed=False, tm=ROW_TILE)[:m_rows]
    kv5 = lambda a: a.reshape(bsz, t, 2, N_KV, hd)
    return y.reshape(bsz, t, d), (kv5(kvc), kv5(kvs), s_win, c, n, m)


def kernel(x_prompt, x_sample, cache_cmp_kv, cache_sel_kv, state_win_kv, state_mlstm_c, state_mlstm_n,
           state_mlstm_m, page_table, g_pre, w_in, b_i, b_f, w_cmp_k, w_cmp_v, w_out, g_post):
    depth = w_in.shape[0]
    d = x_prompt.shape[-1]
    hd = w_cmp_k.shape[-1]
    nw = N_HEADS * hd
    mw = w_out.shape[1] - nw
    kvw = 2 * N_KV * hd
    xp, xs = x_prompt, x_sample
    new_p = [[] for _ in range(6)]
    new_s = [[] for _ in range(6)]
    for layer in range(depth):
        wn, wt, bn, bt = _split_weights(w_in[layer], b_i[layer], b_f[layer], mw, nw, kvw)
        bd = _compress_weights(w_cmp_k[layer], w_cmp_v[layer])
        prm = (g_pre[layer], wn, wt, bn, bt, bd, w_out[layer], g_post[layer], (mw, nw, kvw))
        xp, st_p = _prompt_layer(xp, prm)
        xs, st_s = _sample_layer(xs, cache_cmp_kv[layer], cache_sel_kv[layer],
                                 state_win_kv[layer], state_mlstm_c[layer], state_mlstm_n[layer],
                                 state_mlstm_m[layer], page_table, prm)
        for lst, a in zip(new_p, st_p):
            lst.append(a)
        for lst, a in zip(new_s, st_s):
            lst.append(a)
    return (xp, xs) + tuple(jnp.stack(a) for a in new_p) + tuple(jnp.stack(a) for a in new_s)
```
